```python
import jax
import jax.numpy as jnp
from jax import lax
import numpy as np

D_MODEL = 2048
BATCH = 4
SEQ = 8192
DEPTH = 1

CTX_LEN = 256
GRID_W = 64
WIN_H = 8
WIN_W = 16
NA_HEADS = 16
NA_HEAD_DIM = 64
NA_WIDTH = NA_HEADS * NA_HEAD_DIM
HG_HEADS = 8
HG_KEY_DIM = 128
HG_VAL_DIM = 128
HG_FDIM = HG_HEADS * HG_KEY_DIM
HG_WIDTH = HG_HEADS * HG_VAL_DIM
HG_CHUNK = 64
FFN_HIDDEN = -(-8 * D_MODEL // (3 * 256)) * 256
N_MOD = 6
EPS = 1e-6
IN_SPLIT = (NA_WIDTH, NA_WIDTH, NA_WIDTH, HG_FDIM, HG_FDIM, HG_FDIM, HG_WIDTH, HG_WIDTH, D_MODEL, D_MODEL)
IN_WIDTH = sum(IN_SPLIT)

kernel_name = 'hybrid_natten_hgrn2_dit_layer'


def rmsnorm(x, g):
    xf = x.astype(jnp.float32)
    y = xf * lax.rsqrt(jnp.mean(xf * xf, axis=-1, keepdims=True) + EPS)
    return (y * g.astype(jnp.float32)).astype(x.dtype)


def modulate(h, shift, scale):
    return h * (1 + scale) + shift


def split_columns(u):
    cuts, acc = [], 0
    for w in IN_SPLIT[:-1]:
        acc += w
        cuts.append(acc)
    return jnp.split(u, cuts, axis=-1)


def to_heads(a, n_heads):
    return a.reshape(a.shape[0], a.shape[1], n_heads, -1)


def neighbourhood_attention(q, k, v, k_ctx, v_ctx, rpb):
    b, s, h, dh = q.shape
    rows = s // GRID_W
    kh = min(WIN_H, rows)
    scale = dh ** -0.5
    qg = q.reshape(b, rows, GRID_W, h, dh)
    kg = k.reshape(b, rows, GRID_W, h, dh)
    vg = v.reshape(b, rows, GRID_W, h, dh)
    col = jnp.arange(GRID_W)
    col_start = jnp.clip(col - WIN_W // 2, 0, GRID_W - WIN_W)
    in_win = (col[None, :] >= col_start[:, None]) & (col[None, :] < col_start[:, None] + WIN_W)
    band_mask = jnp.broadcast_to(in_win[:, None, :], (GRID_W, kh, GRID_W)).reshape(GRID_W, kh * GRID_W)
    dc_idx = jnp.clip(col[None, :] - col[:, None], 1 - WIN_W, WIN_W - 1) + WIN_W - 1
    row_start = jnp.clip(jnp.arange(rows) - kh // 2, 0, rows - kh)
    nb = kh * GRID_W

    def one_row(r):
        rs = row_start[r]
        q_r = lax.dynamic_index_in_dim(qg, r, axis=1, keepdims=False)
        k_band = lax.dynamic_slice_in_dim(kg, rs, kh, axis=1).reshape(b, nb, h, dh)
        v_band = lax.dynamic_slice_in_dim(vg, rs, kh, axis=1).reshape(b, nb, h, dh)
        dr_idx = rs + jnp.arange(kh) - r + WIN_H - 1
        bias = rpb[:, dr_idx][:, :, dc_idx]
        bias = bias.transpose(0, 2, 1, 3).reshape(h, GRID_W, nb).astype(jnp.float32)
        s_band = jnp.einsum('bqhd,bnhd->bhqn', q_r, k_band, preferred_element_type=jnp.float32) * scale + bias
        s_band = jnp.where(band_mask, s_band, -jnp.inf)
        s_ctx = jnp.einsum('bqhd,bmhd->bhqm', q_r, k_ctx, preferred_element_type=jnp.float32) * scale
        p = jax.nn.softmax(jnp.concatenate([s_band, s_ctx], axis=-1), axis=-1).astype(v.dtype)
        return (jnp.einsum('bhqn,bnhd->bqhd', p[..., :nb], v_band)
                + jnp.einsum('bhqm,bmhd->bqhd', p[..., nb:], v_ctx))

    out = lax.map(one_row, jnp.arange(rows))
    return out.transpose(1, 0, 2, 3, 4).reshape(b, s, h * dh)


def context_attention(q, k, v):
    b, l, h, dh = q.shape
    s = jnp.einsum('blhd,bmhd->bhlm', q, k, preferred_element_type=jnp.float32) * dh ** -0.5
    p = jax.nn.softmax(s, axis=-1).astype(v.dtype)
    return jnp.einsum('bhlm,bmhd->blhd', p, v).reshape(b, l, h * dh)


def hgrn2_gates(f_logits, lb):
    f = lb + (1 - lb) * jax.nn.sigmoid(f_logits.astype(jnp.float32))
    return jnp.log(f), 1 - f


def hgrn2_chunk_scan(q, k, log_f, v, state0):
    b, t, h, dk = q.shape
    dv = v.shape[-1]
    n = t // HG_CHUNK

    def chunks(a):
        return a.astype(jnp.float32).reshape(b, n, HG_CHUNK, h, a.shape[-1]).transpose(1, 0, 3, 2, 4)

    tri = jnp.tril(jnp.ones((HG_CHUNK, HG_CHUNK), dtype=bool))[:, :, None]

    def step(state, inp):
        q_c, k_c, lf_c, v_c = inp
        cum = jnp.cumsum(lf_c, axis=2)
        rel = jnp.exp(jnp.where(tri, cum[:, :, :, None, :] - cum[:, :, None, :, :], -jnp.inf))
        attn = jnp.einsum('bhtk,bhsk,bhtsk->bhts', q_c, k_c, rel)
        out = (jnp.einsum('bhts,bhsv->bhtv', attn, v_c)
               + jnp.einsum('bhtk,bhkv->bhtv', q_c * jnp.exp(cum), state))
        last = cum[:, :, -1:, :]
        new_state = (jnp.exp(last[:, :, 0, :])[..., None] * state
                     + jnp.einsum('bhsk,bhsv->bhkv', k_c * jnp.exp(last - cum), v_c))
        return new_state, out

    final, out = lax.scan(step, state0, (chunks(q), chunks(k), chunks(log_f), chunks(v)))
    return out.transpose(1, 0, 3, 2, 4).reshape(b, t, h, dv), final


def hgrn2_bidirectional(q, f_fwd, f_bwd, i, lb_fwd, lb_bwd, state_fwd, state_bwd):
    lf1, k1 = hgrn2_gates(f_fwd, lb_fwd)
    lf2, k2 = hgrn2_gates(f_bwd, lb_bwd)
    o1, s1 = hgrn2_chunk_scan(q, k1, lf1, i, state_fwd)
    o2, s2 = hgrn2_chunk_scan(jnp.flip(q, 1), jnp.flip(k2, 1), jnp.flip(lf2, 1), jnp.flip(i, 1), state_bwd)
    return o1 + jnp.flip(o2, 1), s1, s2


def hgrn2_readout(o, out_gate, g):
    on = o * lax.rsqrt(jnp.mean(o * o, axis=-1, keepdims=True) + EPS) * g.astype(jnp.float32)
    on = on.reshape(o.shape[0], o.shape[1], -1)
    return (on * jax.nn.silu(out_gate.astype(jnp.float32))).astype(out_gate.dtype)


def gated_merge(o_a, o_b, gate_a, gate_b, w_pa, w_pb, w_out):
    y = jax.nn.sigmoid(gate_a) * (o_a @ w_pa) + jax.nn.sigmoid(gate_b) * (o_b @ w_pb)
    return y @ w_out


def swiglu(h, w_in, w_out):
    a, u = jnp.split(h @ w_in, 2, axis=-1)
    return (jax.nn.silu(a) * u) @ w_out


def token_mixing(h, hc, w_in, rpb, lb_f, lb_b, hg_g, w_pa, w_pb, w_out, with_ctx_out):
    q, k, v, hq, hf_f, hf_b, hi, hog, ga, gb = split_columns(h @ w_in)
    qc, kc, vc, hqc, hfc_f, hfc_b, hic, hogc, gac, gbc = split_columns(hc @ w_in)
    b = hc.shape[0]
    kc_h, vc_h = to_heads(kc, NA_HEADS), to_heads(vc, NA_HEADS)
    o_na = neighbourhood_attention(to_heads(q, NA_HEADS), to_heads(k, NA_HEADS), to_heads(v, NA_HEADS),
                                   kc_h, vc_h, rpb)
    zero = jnp.zeros((b, HG_HEADS, HG_KEY_DIM, HG_VAL_DIM), jnp.float32)
    oc_hg, s_f, s_b = hgrn2_bidirectional(to_heads(hqc, HG_HEADS), to_heads(hfc_f, HG_HEADS),
                                          to_heads(hfc_b, HG_HEADS), to_heads(hic, HG_HEADS),
                                          lb_f, lb_b, zero, zero)
    o_hg, _, _ = hgrn2_bidirectional(to_heads(hq, HG_HEADS), to_heads(hf_f, HG_HEADS),
                                     to_heads(hf_b, HG_HEADS), to_heads(hi, HG_HEADS),
                                     lb_f, lb_b, s_f, s_b)
    y = gated_merge(o_na, hgrn2_readout(o_hg, hog, hg_g), ga, gb, w_pa, w_pb, w_out)
    yc = None
    if with_ctx_out:
        oc_na = context_attention(to_heads(qc, NA_HEADS), kc_h, vc_h)
        yc = gated_merge(oc_na, hgrn2_readout(oc_hg, hogc, hg_g), gac, gbc, w_pa, w_pb, w_out)
    return y, yc


def setup_inputs(seed: int = 0) -> dict:
    key = jax.random.key(seed)
    ks = jax.random.split(key, 20)
    f32 = jnp.float32

    def nrm(k, shape, scale):
        return jax.random.normal(k, shape, f32) * scale

    return {
        'x': nrm(ks[0], (BATCH, SEQ, D_MODEL), 1.0),
        'c': nrm(ks[1], (BATCH, D_MODEL), 1.0),
        'ctx': nrm(ks[2], (BATCH, CTX_LEN, D_MODEL), 1.0),
        'c_ctx': nrm(ks[3], (D_MODEL,), 1.0),
        'w_ada': nrm(ks[4], (DEPTH, D_MODEL, N_MOD * D_MODEL), 0.5 * D_MODEL ** -0.5),
        'b_ada': nrm(ks[5], (DEPTH, N_MOD * D_MODEL), 0.01),
        'norm1_g': 1.0 + nrm(ks[6], (DEPTH, D_MODEL), 0.02),
        'w_in': nrm(ks[7], (DEPTH, D_MODEL, IN_WIDTH), D_MODEL ** -0.5),
        'na_rpb': nrm(ks[8], (DEPTH, NA_HEADS, 2 * WIN_H - 1, 2 * WIN_W - 1), 0.1),
        'hg_lb_logits': nrm(ks[9], (DEPTH + 1, 2, HG_FDIM), 1.0),
        'hg_norm_g': 1.0 + nrm(ks[10], (DEPTH, HG_VAL_DIM), 0.02),
        'w_pa': nrm(ks[11], (DEPTH, NA_WIDTH, D_MODEL), NA_WIDTH ** -0.5),
        'w_pb': nrm(ks[12], (DEPTH, HG_WIDTH, D_MODEL), HG_WIDTH ** -0.5),
        'w_out': nrm(ks[13], (DEPTH, D_MODEL, D_MODEL), D_MODEL ** -0.5),
        'norm2_g': 1.0 + nrm(ks[14], (DEPTH, D_MODEL), 0.02),
        'w_ffn_in': nrm(ks[15], (DEPTH, D_MODEL, 2 * FFN_HIDDEN), D_MODEL ** -0.5),
        'w_ffn_out': nrm(ks[16], (DEPTH, FFN_HIDDEN, D_MODEL), FFN_HIDDEN ** -0.5),
        'final_g': 1.0 + nrm(ks[17], (D_MODEL,), 0.02),
    }


def reference(x, c, ctx, c_ctx, w_ada, b_ada, norm1_g, w_in, na_rpb, hg_lb_logits, hg_norm_g,
              w_pa, w_pb, w_out, norm2_g, w_ffn_in, w_ffn_out, final_g):
    b = x.shape[0]
    lb_table = jnp.cumsum(jax.nn.softmax(hg_lb_logits.astype(jnp.float32), axis=0), axis=0)
    silu_c = jax.nn.silu(c)
    silu_cc = jax.nn.silu(c_ctx)
    xc = ctx
    for l in range(DEPTH):
        last = l == DEPTH - 1
        mod = (silu_c @ w_ada[l] + b_ada[l]).reshape(b, N_MOD, 1, D_MODEL)
        mod_c = (silu_cc @ w_ada[l] + b_ada[l]).reshape(N_MOD, 1, D_MODEL)
        sh1, sc1, g1, sh2, sc2, g2 = [mod[:, j] for j in range(N_MOD)]
        sh1c, sc1c, g1c, sh2c, sc2c, g2c = [mod_c[j] for j in range(N_MOD)]
        lb_f = lb_table[l, 0].reshape(HG_HEADS, HG_KEY_DIM)
        lb_b = lb_table[l, 1].reshape(HG_HEADS, HG_KEY_DIM)
        h = modulate(rmsnorm(x, norm1_g[l]), sh1, sc1)
        hc = modulate(rmsnorm(xc, norm1_g[l]), sh1c, sc1c)
        y, yc = token_mixing(h, hc, w_in[l], na_rpb[l], lb_f, lb_b, hg_norm_g[l],
                             w_pa[l], w_pb[l], w_out[l], not last)
        x = x + g1 * y
        x = x + g2 * swiglu(modulate(rmsnorm(x, norm2_g[l]), sh2, sc2), w_ffn_in[l], w_ffn_out[l])
        if not last:
            xc = xc + g1c * yc
            xc = xc + g2c * swiglu(modulate(rmsnorm(xc, norm2_g[l]), sh2c, sc2c), w_ffn_in[l], w_ffn_out[l])
    return rmsnorm(x, final_g)
```

```python
import functools

import jax
import jax.numpy as jnp
import numpy as np
from jax import lax
from jax.experimental import pallas as pl
from jax.experimental.pallas import tpu as pltpu

F32 = jnp.float32
BF16 = jnp.bfloat16

GRID_W = 64
WIN_H = 8
WIN_W = 16
NA_HEADS = 16
NA_HEAD_DIM = 64
NA_WIDTH = NA_HEADS * NA_HEAD_DIM
HG_HEADS = 8
HG_DIM = 128
HG_WIDTH = HG_HEADS * HG_DIM
N_MOD = 6
EPS = 1e-6
NEG_BIG = -1e30

V7X_VMEM_LIMIT_BYTES = 56 * 1024 * 1024
HG_CHUNK = 64
HG_LEVELS = 6


def _cparams(sem):
    return pltpu.CompilerParams(dimension_semantics=sem, vmem_limit_bytes=V7X_VMEM_LIMIT_BYTES)


def _pick(n, prefs):
    for p in prefs:
        if n % p == 0:
            return p
    return n


def _ada_kernel(c_ref, w_ref, b_ref, o_ref):
    c = c_ref[...]
    s = c * jax.nn.sigmoid(c)
    o_ref[...] = jnp.dot(s, w_ref[...], precision=lax.Precision.HIGHEST,
                         preferred_element_type=F32) + b_ref[...]


def _ada(c8, w, b):
    d, n = w.shape
    tn = _pick(n, (1024, 512, 256, 128))
    return pl.pallas_call(
        _ada_kernel,
        grid=(n // tn,),
        in_specs=[pl.BlockSpec((8, d), lambda j: (0, 0)),
                  pl.BlockSpec((d, tn), lambda j: (0, j)),
                  pl.BlockSpec((1, tn), lambda j: (0, j))],
        out_specs=pl.BlockSpec((8, tn), lambda j: (0, j)),
        out_shape=jax.ShapeDtypeStruct((8, n), F32),
        compiler_params=_cparams(("arbitrary",)),
        name="ada",
    )(c8, w, b.reshape(1, n))


def _inproj_kernel(n_qkv, n_mix, x_ref, g_ref, sh_ref, sc_ref, w_ref, qkv_ref, mix_ref, fg_ref, h_scr):
    j = pl.program_id(2)

    @pl.when(j == 0)
    def _():
        x = x_ref[0]
        ms = jnp.mean(x * x, axis=-1, keepdims=True)
        y = x * lax.rsqrt(ms + EPS) * g_ref[...]
        h_scr[...] = (y * (1.0 + sc_ref[0]) + sh_ref[0]).astype(BF16)

    u = jnp.dot(h_scr[...], w_ref[...], preferred_element_type=F32)

    @pl.when(j < n_qkv)
    def _():
        qkv_ref[0] = u.astype(BF16)

    @pl.when((j >= n_qkv) & (j < n_qkv + n_mix))
    def _():
        mix_ref[0] = u.astype(BF16)

    @pl.when(j >= n_qkv + n_mix)
    def _():
        fg_ref[0] = u


def _inproj(x, g, sh, sc, w, d):
    b, t, _ = x.shape
    w_qkv, w_mix, w_fg = 3 * NA_WIDTH, 2 * d + 3 * HG_WIDTH, 2 * HG_WIDTH
    tn = 1024
    assert w_qkv % tn == 0 and w_mix % tn == 0 and w_fg % tn == 0
    n_qkv, n_mix, n_fg = w_qkv // tn, w_mix // tn, w_fg // tn
    tm = _pick(t, (512, 256))
    return pl.pallas_call(
        functools.partial(_inproj_kernel, n_qkv, n_mix),
        grid=(b, t // tm, n_qkv + n_mix + n_fg),
        in_specs=[pl.BlockSpec((1, tm, d), lambda bi, i, j: (bi, i, 0)),
                  pl.BlockSpec((1, d), lambda bi, i, j: (0, 0)),
                  pl.BlockSpec((1, 1, d), lambda bi, i, j: (bi, 0, 0)),
                  pl.BlockSpec((1, 1, d), lambda bi, i, j: (bi, 0, 0)),
                  pl.BlockSpec((d, tn), lambda bi, i, j: (0, j))],
        out_specs=[pl.BlockSpec((1, tm, tn), lambda bi, i, j: (bi, i, jnp.minimum(j, n_qkv - 1))),
                   pl.BlockSpec((1, tm, tn), lambda bi, i, j: (bi, i, jnp.clip(j - n_qkv, 0, n_mix - 1))),
                   pl.BlockSpec((1, tm, tn), lambda bi, i, j: (bi, i, jnp.maximum(j - n_qkv - n_mix, 0)))],
        out_shape=[jax.ShapeDtypeStruct((b, t, w_qkv), BF16),
                   jax.ShapeDtypeStruct((b, t, w_mix), BF16),
                   jax.ShapeDtypeStruct((b, t, w_fg), F32)],
        scratch_shapes=[pltpu.VMEM((tm, d), BF16)],
        compiler_params=_cparams(("parallel", "parallel", "arbitrary")),
        name="inproj",
    )(x, g, sh, sc, w)


def _na_kernel(nb, q_ref, kb_ref, vb_ref, kc_ref, vc_ref, bias_ref, o_ref):
    dn = (((1,), (1,)), ((), ()))
    for h in range(NA_HEADS):
        sl = slice(h * NA_HEAD_DIM, (h + 1) * NA_HEAD_DIM)
        q = q_ref[0, :, sl] * (NA_HEAD_DIM ** -0.5)
        s_b = lax.dot_general(q, kb_ref[0, :, sl], dn, preferred_element_type=F32) + bias_ref[0, h]
        s_c = lax.dot_general(q, kc_ref[0, :, sl], dn, preferred_element_type=F32)
        m = jnp.maximum(jnp.max(s_b, axis=-1, keepdims=True), jnp.max(s_c, axis=-1, keepdims=True))
        p_b = jnp.exp(s_b - m)
        p_c = jnp.exp(s_c - m)
        l = jnp.sum(p_b, axis=-1, keepdims=True) + jnp.sum(p_c, axis=-1, keepdims=True)
        o = (jnp.dot(p_b.astype(BF16), vb_ref[0, :, sl], preferred_element_type=F32)
             + jnp.dot(p_c.astype(BF16), vc_ref[0, :, sl], preferred_element_type=F32))
        o_ref[0, :, sl] = (o / l).astype(BF16)


def _na_bias_table(rpb):
    col = np.arange(GRID_W)
    col_start = np.clip(col - WIN_W // 2, 0, GRID_W - WIN_W)
    in_win = (col[None, :] >= col_start[:, None]) & (col[None, :] < col_start[:, None] + WIN_W)
    dc_idx = np.clip(col[None, :] - col[:, None], 1 - WIN_W, WIN_W - 1) + WIN_W - 1
    g = rpb.astype(F32)[:, :, dc_idx]
    g = jnp.where(jnp.asarray(in_win)[None, None], g, NEG_BIG)
    tabs = [g[:, off:off + WIN_H].transpose(0, 2, 1, 3).reshape(NA_HEADS, GRID_W, WIN_H * GRID_W)
            for off in range(WIN_H)]
    return jnp.stack(tabs)


def _na(qkv, qkv_c, bias_tab):
    b, s, _ = qkv.shape
    lc = qkv_c.shape[1]
    rows = s // GRID_W
    assert rows >= WIN_H
    nb = WIN_H * GRID_W

    def rs_of(r):
        return jnp.clip(r - WIN_H // 2, 0, rows - WIN_H)

    band = (pl.Element(1), pl.Element(nb), pl.Element(NA_WIDTH))

    return pl.pallas_call(
        functools.partial(_na_kernel, nb),
        grid=(b, rows),
        in_specs=[pl.BlockSpec((1, GRID_W, NA_WIDTH), lambda bi, r: (bi, r, 0)),
                  pl.BlockSpec(band, lambda bi, r: (bi, rs_of(r) * GRID_W, NA_WIDTH)),
                  pl.BlockSpec(band, lambda bi, r: (bi, rs_of(r) * GRID_W, 2 * NA_WIDTH)),
                  pl.BlockSpec((1, lc, NA_WIDTH), lambda bi, r: (bi, 0, 1)),
                  pl.BlockSpec((1, lc, NA_WIDTH), lambda bi, r: (bi, 0, 2)),
                  pl.BlockSpec((1, NA_HEADS, GRID_W, nb),
                               lambda bi, r: (rs_of(r) - r + WIN_H - 1, 0, 0, 0))],
        out_specs=pl.BlockSpec((1, GRID_W, NA_WIDTH), lambda bi, r: (bi, r, 0)),
        out_shape=jax.ShapeDtypeStruct((b, s, NA_WIDTH), BF16),
        compiler_params=_cparams(("parallel", "arbitrary")),
        name="na",
    )(qkv, qkv, qkv, qkv_c, qkv_c, bias_tab)


def _hg_constants(reverse):
    c = HG_CHUNK
    pos = np.arange(c)
    dmat = np.zeros((HG_LEVELS + 2, c, c), np.float32)
    qsel = np.zeros((HG_LEVELS, c, 1), np.float32)
    amask = np.zeros((HG_LEVELS + 1, c, c), np.float32)
    for l in range(HG_LEVELS):
        m = 1 << l
        blk = pos // (2 * m)
        bnd = blk * 2 * m + m - 1
        is_q = (pos % (2 * m)) >= m
        qsel[l, :, 0] = is_q
        for t in range(c):
            if is_q[t]:
                dmat[l, t, bnd[t] + 1:t + 1] = 1.0
            else:
                dmat[l, t, t + 1:bnd[t] + 1] = 1.0
        amask[l] = (blk[:, None] == blk[None, :]) & is_q[:, None] & ~is_q[None, :]
    for t in range(c):
        dmat[HG_LEVELS, t, :t + 1] = 1.0
        dmat[HG_LEVELS + 1, t, t + 1:] = 1.0
    amask[HG_LEVELS] = np.eye(c)
    if reverse:
        dmat = dmat[:, ::-1, ::-1]
        qsel = qsel[:, ::-1]
        amask = amask[:, ::-1, ::-1]
    return (jnp.asarray(dmat.reshape((HG_LEVELS + 2) * c, c), BF16),
            jnp.asarray(qsel, F32), jnp.asarray(amask, F32))


def _split3(a):
    hi = a.astype(BF16)
    r1 = a - hi.astype(F32)
    mid = r1.astype(BF16)
    lo = (r1 - mid.astype(F32)).astype(BF16)
    return hi, mid, lo


def _hg_chunk(q, z, v, lb, st, dmat, qsel_ref, amask_ref, reverse):
    c = HG_CHUNK
    dn_nt = (((1,), (1,)), ((), ()))
    f = lb + (1.0 - lb) * jax.nn.sigmoid(z)
    lf = jnp.log(f)
    kk = 1.0 - f
    qf = q.astype(F32)
    hi, mid, lo = _split3(lf)
    expo = (jnp.dot(dmat, hi, preferred_element_type=F32)
            + jnp.dot(dmat, mid, preferred_element_type=F32)
            + jnp.dot(dmat, lo, preferred_element_type=F32))
    e = jnp.exp(expo)
    a = jnp.zeros((c, c), F32)
    for l in range(HG_LEVELS):
        el = e[l * c:(l + 1) * c]
        sel = qsel_ref[l]
        ql = (qf * el * sel).astype(BF16)
        kl = (kk * el * (1.0 - sel)).astype(BF16)
        a = a + amask_ref[l] * lax.dot_general(ql, kl, dn_nt, preferred_element_type=F32)
    a = a + amask_ref[HG_LEVELS] * lax.dot_general(q, kk.astype(BF16), dn_nt, preferred_element_type=F32)
    e_in = e[HG_LEVELS * c:(HG_LEVELS + 1) * c]
    e_out = e[(HG_LEVELS + 1) * c:(HG_LEVELS + 2) * c]
    o = (jnp.dot(a.astype(BF16), v, preferred_element_type=F32)
         + lax.dot_general((qf * e_in).astype(BF16), st.astype(BF16), dn_nt, preferred_element_type=F32))
    tot = e_in[0:1] if reverse else e_in[c - 1:c]
    upd = lax.dot_general(v, (kk * e_out).astype(BF16), (((0,), (0,)), ((), ())),
                          preferred_element_type=F32)
    return o, st * tot + upd


def _hg_kernel(reverse, fuse_readout, nchunk, *refs):
    if fuse_readout:
        (q_ref, v_ref, z_ref, lb_ref, s0_ref, dmat_ref, qsel_ref, amask_ref, of_ref, og_ref, gn_ref,
         o_ref, sout_ref, st_scr) = refs
    else:
        (q_ref, v_ref, z_ref, lb_ref, s0_ref, dmat_ref, qsel_ref, amask_ref,
         o_ref, sout_ref, st_scr) = refs
    t = pl.program_id(2)

    @pl.when(t == 0)
    def _():
        st_scr[...] = s0_ref[0, 0]

    c = HG_CHUNK
    lb = lb_ref[0]
    dmat = dmat_ref[...]
    st = st_scr[...]
    order = range(nchunk - 1, -1, -1) if reverse else range(nchunk)
    for ci in order:
        rs = slice(ci * c, (ci + 1) * c)
        o, st = _hg_chunk(q_ref[0, rs], z_ref[0, rs], v_ref[0, rs], lb, st, dmat, qsel_ref, amask_ref, reverse)
        if fuse_readout:
            o = o + of_ref[0, rs]
            on = o * lax.rsqrt(jnp.mean(o * o, axis=-1, keepdims=True) + EPS) * gn_ref[...]
            g = og_ref[0, rs].astype(F32)
            o_ref[0, rs] = (on * (g * jax.nn.sigmoid(g))).astype(o_ref.dtype)
        else:
            o_ref[0, rs] = o
    st_scr[...] = st

    @pl.when(t == pl.num_programs(2) - 1)
    def _():
        sout_ref[0, 0] = st


def _hgrn(mix, fg, lb, s0, consts, reverse, d, o_fwd=None, gn=None):
    b, s, _ = mix.shape
    tt = _pick(s, (256, 128, 64))
    nt = s // tt
    base = 2 * d // HG_DIM
    fuse = o_fwd is not None
    dmat, qsel, amask = consts
    fcol = HG_HEADS if reverse else 0

    def tmap(t):
        return nt - 1 - t if reverse else t

    blk = (1, tt, HG_DIM)
    in_specs = [pl.BlockSpec(blk, lambda bi, h, t: (bi, tmap(t), base + h)),
                pl.BlockSpec(blk, lambda bi, h, t: (bi, tmap(t), base + HG_HEADS + h)),
                pl.BlockSpec(blk, lambda bi, h, t: (bi, tmap(t), fcol + h)),
                pl.BlockSpec((1, 1, HG_DIM), lambda bi, h, t: (h, 0, 0)),
                pl.BlockSpec((1, 1, HG_DIM, HG_DIM), lambda bi, h, t: (bi, h, 0, 0)),
                pl.BlockSpec(dmat.shape, lambda bi, h, t: (0, 0)),
                pl.BlockSpec(qsel.shape, lambda bi, h, t: (0, 0, 0)),
                pl.BlockSpec(amask.shape, lambda bi, h, t: (0, 0, 0))]
    args = [mix, mix, fg, lb.reshape(HG_HEADS, 1, HG_DIM), s0, dmat, qsel, amask]
    if fuse:
        in_specs += [pl.BlockSpec(blk, lambda bi, h, t: (bi, tmap(t), h)),
                     pl.BlockSpec(blk, lambda bi, h, t: (bi, tmap(t), base + 2 * HG_HEADS + h)),
                     pl.BlockSpec((1, HG_DIM), lambda bi, h, t: (0, 0))]
        args += [o_fwd, mix, gn.reshape(1, HG_DIM)]
    return pl.pallas_call(
        functools.partial(_hg_kernel, reverse, fuse, tt // HG_CHUNK),
        grid=(b, HG_HEADS, nt),
        in_specs=in_specs,
        out_specs=[pl.BlockSpec(blk, lambda bi, h, t: (bi, tmap(t), h)),
                   pl.BlockSpec((1, 1, HG_DIM, HG_DIM), lambda bi, h, t: (bi, h, 0, 0))],
        out_shape=[jax.ShapeDtypeStruct((b, s, HG_WIDTH), BF16 if fuse else F32),
                   jax.ShapeDtypeStruct((b, HG_HEADS, HG_DIM, HG_DIM), F32)],
        scratch_shapes=[pltpu.VMEM((HG_DIM, HG_DIM), F32)],
        compiler_params=_cparams(("parallel", "parallel", "arbitrary")),
        name="hgrn_bwd" if reverse else "hgrn_fwd",
    )(*args)


def _merge_kernel(oa_ref, ob_ref, ga_ref, gb_ref, x_ref, g1_ref, n2_ref, sh2_ref, sc2_ref,
                  wpa_ref, wpb_ref, wo_ref, x1_ref, h2_ref):
    ya = jnp.dot(oa_ref[0], wpa_ref[...], preferred_element_type=F32)
    yb = jnp.dot(ob_ref[0], wpb_ref[...], preferred_element_type=F32)
    y = (jax.nn.sigmoid(ga_ref[0].astype(F32)) * ya + jax.nn.sigmoid(gb_ref[0].astype(F32)) * yb)
    z = jnp.dot(y.astype(BF16), wo_ref[...], preferred_element_type=F32)
    x1 = x_ref[0] + g1_ref[0] * z
    x1_ref[0] = x1
    ms = jnp.mean(x1 * x1, axis=-1, keepdims=True)
    hn = x1 * lax.rsqrt(ms + EPS) * n2_ref[...]
    h2_ref[0] = (hn * (1.0 + sc2_ref[0]) + sh2_ref[0]).astype(BF16)


def _merge(o_a, o_b, mix, x, g1, n2, sh2, sc2, wpa, wpb, wo):
    b, s, d = x.shape
    tm = _pick(s, (256, 128))
    const = lambda shape: pl.BlockSpec(shape, lambda bi, i: (0,) * len(shape), pipeline_mode=pl.Buffered(1))
    vec = pl.BlockSpec((1, 1, d), lambda bi, i: (bi, 0, 0))
    return pl.pallas_call(
        _merge_kernel,
        grid=(b, s // tm),
        in_specs=[pl.BlockSpec((1, tm, NA_WIDTH), lambda bi, i: (bi, i, 0)),
                  pl.BlockSpec((1, tm, HG_WIDTH), lambda bi, i: (bi, i, 0)),
                  pl.BlockSpec((1, tm, d), lambda bi, i: (bi, i, 0)),
                  pl.BlockSpec((1, tm, d), lambda bi, i: (bi, i, 1)),
                  pl.BlockSpec((1, tm, d), lambda bi, i: (bi, i, 0)),
                  vec, const((1, d)), vec, vec,
                  const(wpa.shape), const(wpb.shape), const(wo.shape)],
        out_specs=[pl.BlockSpec((1, tm, d), lambda bi, i: (bi, i, 0)),
                   pl.BlockSpec((1, tm, d), lambda bi, i: (bi, i, 0))],
        out_shape=[jax.ShapeDtypeStruct((b, s, d), F32), jax.ShapeDtypeStruct((b, s, d), BF16)],
        compiler_params=_cparams(("parallel", "parallel")),
        name="merge",
    )(o_a, o_b, mix, mix, x, g1, n2, sh2, sc2, wpa, wpb, wo)


def _ffn_kernel(h_ref, x1_ref, g2_ref, fg_ref, wa_ref, wu_ref, wo_ref, o_ref, acc):
    j = pl.program_id(2)

    @pl.when(j == 0)
    def _():
        acc[...] = jnp.zeros_like(acc)

    h = h_ref[0]
    a = jnp.dot(h, wa_ref[...], preferred_element_type=F32)
    u = jnp.dot(h, wu_ref[...], preferred_element_type=F32)
    gate = (a * jax.nn.sigmoid(a) * u).astype(BF16)
    acc[...] += jnp.dot(gate, wo_ref[...], preferred_element_type=F32)

    @pl.when(j == pl.num_programs(2) - 1)
    def _():
        x2 = x1_ref[0] + g2_ref[0] * acc[...]
        ms = jnp.mean(x2 * x2, axis=-1, keepdims=True)
        o_ref[0] = x2 * lax.rsqrt(ms + EPS) * fg_ref[...]


def _ffn(h2, x1, g2, final_g, w_in, w_out):
    b, s, d = x1.shape
    fh = w_out.shape[0]
    tf = _pick(fh, (512, 256, 128))
    nf = fh // tf
    tm = _pick(s, (512, 256, 128))
    return pl.pallas_call(
        _ffn_kernel,
        grid=(b, s // tm, nf),
        in_specs=[pl.BlockSpec((1, tm, d), lambda bi, i, j: (bi, i, 0)),
                  pl.BlockSpec((1, tm, d), lambda bi, i, j: (bi, i, 0)),
                  pl.BlockSpec((1, 1, d), lambda bi, i, j: (bi, 0, 0)),
                  pl.BlockSpec((1, d), lambda bi, i, j: (0, 0)),
                  pl.BlockSpec((d, tf), lambda bi, i, j: (0, j)),
                  pl.BlockSpec((d, tf), lambda bi, i, j: (0, nf + j)),
                  pl.BlockSpec((tf, d), lambda bi, i, j: (j, 0))],
        out_specs=pl.BlockSpec((1, tm, d), lambda bi, i, j: (bi, i, 0)),
        out_shape=jax.ShapeDtypeStruct((b, s, d), F32),
        scratch_shapes=[pltpu.VMEM((tm, d), F32)],
        compiler_params=_cparams(("parallel", "parallel", "arbitrary")),
        name="ffn",
    )(h2, x1, g2, final_g.reshape(1, d), w_in, w_in, w_out)


def kernel(x, c, ctx, c_ctx, w_ada, b_ada, norm1_g, w_in, na_rpb, hg_lb_logits, hg_norm_g,
           w_pa, w_pb, w_out, norm2_g, w_ffn_in, w_ffn_out, final_g):
    b, s, d = x.shape
    assert w_ada.shape[0] == 1, "single layer"

    lb_table = jnp.cumsum(jax.nn.softmax(hg_lb_logits.astype(F32), axis=0), axis=0)
    lb_f, lb_b = lb_table[0, 0], lb_table[0, 1]

    c8 = jnp.zeros((8, d), F32).at[:b].set(c).at[b].set(c_ctx)
    mod = _ada(c8, w_ada[0], b_ada[0]).reshape(8, N_MOD, d)
    sh1, sc1, g1, sh2, sc2, g2 = [mod[:b, j][:, None, :] for j in range(N_MOD)]
    sh1c = jnp.broadcast_to(mod[b, 0][None, None, :], (b, 1, d))
    sc1c = jnp.broadcast_to(mod[b, 1][None, None, :], (b, 1, d))

    o = np.cumsum((0, NA_WIDTH, NA_WIDTH, NA_WIDTH, HG_WIDTH, HG_WIDTH, HG_WIDTH, HG_WIDTH, HG_WIDTH, d, d))
    seg = lambda i: w_in[0][:, o[i]:o[i + 1]]
    w_perm = jnp.concatenate([seg(0), seg(1), seg(2), seg(8), seg(9), seg(3), seg(6), seg(7), seg(4), seg(5)],
                             axis=1).astype(BF16)
    n1 = norm1_g[0].reshape(1, d)

    qkv, mix, fg = _inproj(x, n1, sh1, sc1, w_perm, d)
    qkv_c, mix_c, fg_c = _inproj(ctx, n1, sh1c, sc1c, w_perm, d)

    o_na = _na(qkv, qkv_c, _na_bias_table(na_rpb[0]))

    cf, cb = _hg_constants(False), _hg_constants(True)
    zero = jnp.zeros((b, HG_HEADS, HG_DIM, HG_DIM), F32)
    _, s_f = _hgrn(mix_c, fg_c, lb_f, zero, cf, False, d)
    _, s_b = _hgrn(mix_c, fg_c, lb_b, zero, cb, True, d)
    o_f, _ = _hgrn(mix, fg, lb_f, s_f, cf, False, d)
    o_hg, _ = _hgrn(mix, fg, lb_b, s_b, cb, True, d, o_fwd=o_f, gn=hg_norm_g[0])

    x1, h2 = _merge(o_na, o_hg, mix, x, g1, norm2_g[0].reshape(1, d), sh2, sc2,
                    w_pa[0].astype(BF16), w_pb[0].astype(BF16), w_out[0].astype(BF16))
    return _ffn(h2, x1, g2, final_g, w_ffn_in[0].astype(BF16), w_ffn_out[0].astype(BF16))
```

```python
import functools

import jax
import jax.numpy as jnp
import numpy as np
from jax import lax
from jax.experimental import pallas as pl
from jax.experimental.pallas import tpu as pltpu

F32 = jnp.float32
BF16 = jnp.bfloat16

GRID_W = 64
WIN_H = 8
WIN_W = 16
NA_HEADS = 16
NA_HEAD_DIM = 64
NA_WIDTH = NA_HEADS * NA_HEAD_DIM
HG_HEADS = 8
HG_DIM = 128
HG_WIDTH = HG_HEADS * HG_DIM
N_MOD = 6
EPS = 1e-6
NEG_BIG = -1e30

V7X_VMEM_LIMIT_BYTES = 56 * 1024 * 1024
HG_CHUNK = 64
HG_LEVELS = 6


def _cparams(sem):
    return pltpu.CompilerParams(dimension_semantics=sem, vmem_limit_bytes=V7X_VMEM_LIMIT_BYTES)


def _pick(n, prefs):
    for p in prefs:
        if n % p == 0:
            return p
    return n


def _ada_kernel(c_ref, w_ref, b_ref, o_ref):
    c = c_ref[...]
    s = c * jax.nn.sigmoid(c)
    o_ref[...] = jnp.dot(s, w_ref[...], precision=lax.Precision.HIGHEST,
                         preferred_element_type=F32) + b_ref[...]


def _ada(c8, w, b):
    d, n = w.shape
    tn = _pick(n, (1024, 512, 256, 128))
    return pl.pallas_call(
        _ada_kernel,
        grid=(n // tn,),
        in_specs=[pl.BlockSpec((8, d), lambda j: (0, 0)),
                  pl.BlockSpec((d, tn), lambda j: (0, j)),
                  pl.BlockSpec((1, tn), lambda j: (0, j))],
        out_specs=pl.BlockSpec((8, tn), lambda j: (0, j)),
        out_shape=jax.ShapeDtypeStruct((8, n), F32),
        compiler_params=_cparams(("arbitrary",)),
        name="ada",
    )(c8, w, b.reshape(1, n))


def _inproj_kernel(n_qkv, n_mix, x_ref, g_ref, sh_ref, sc_ref, w_ref, qkv_ref, mix_ref, fg_ref, h_scr):
    j = pl.program_id(2)

    @pl.when(j == 0)
    def _():
        x = x_ref[0]
        ms = jnp.mean(x * x, axis=-1, keepdims=True)
        y = x * lax.rsqrt(ms + EPS) * g_ref[...]
        h_scr[...] = (y * (1.0 + sc_ref[0]) + sh_ref[0]).astype(BF16)

    u = jnp.dot(h_scr[...], w_ref[...], preferred_element_type=F32)

    @pl.when(j < n_qkv)
    def _():
        qkv_ref[0] = u.astype(BF16)

    @pl.when((j >= n_qkv) & (j < n_qkv + n_mix))
    def _():
        mix_ref[0] = u.astype(BF16)

    @pl.when(j >= n_qkv + n_mix)
    def _():
        fg_ref[0] = u


def _inproj(x, g, sh, sc, w, d):
    b, t, _ = x.shape
    w_qkv, w_mix, w_fg = 3 * NA_WIDTH, 2 * d + 3 * HG_WIDTH, 2 * HG_WIDTH
    tn = 1024
    assert w_qkv % tn == 0 and w_mix % tn == 0 and w_fg % tn == 0
    n_qkv, n_mix, n_fg = w_qkv // tn, w_mix // tn, w_fg // tn
    tm = _pick(t, (512, 256))
    return pl.pallas_call(
        functools.partial(_inproj_kernel, n_qkv, n_mix),
        grid=(b, t // tm, n_qkv + n_mix + n_fg),
        in_specs=[pl.BlockSpec((1, tm, d), lambda bi, i, j: (bi, i, 0)),
                  pl.BlockSpec((1, d), lambda bi, i, j: (0, 0)),
                  pl.BlockSpec((1, 1, d), lambda bi, i, j: (bi, 0, 0)),
                  pl.BlockSpec((1, 1, d), lambda bi, i, j: (bi, 0, 0)),
                  pl.BlockSpec((d, tn), lambda bi, i, j: (0, j))],
        out_specs=[pl.BlockSpec((1, tm, tn), lambda bi, i, j: (bi, i, jnp.minimum(j, n_qkv - 1))),
                   pl.BlockSpec((1, tm, tn), lambda bi, i, j: (bi, i, jnp.clip(j - n_qkv, 0, n_mix - 1))),
                   pl.BlockSpec((1, tm, tn), lambda bi, i, j: (bi, i, jnp.maximum(j - n_qkv - n_mix, 0)))],
        out_shape=[jax.ShapeDtypeStruct((b, t, w_qkv), BF16),
                   jax.ShapeDtypeStruct((b, t, w_mix), BF16),
                   jax.ShapeDtypeStruct((b, t, w_fg), F32)],
        scratch_shapes=[pltpu.VMEM((tm, d), BF16)],
        compiler_params=_cparams(("parallel", "parallel", "arbitrary")),
        name="inproj",
    )(x, g, sh, sc, w)


def _na_kernel(nb, q_ref, kb_ref, vb_ref, kc_ref, vc_ref, bias_ref, o_ref):
    dn = (((1,), (1,)), ((), ()))
    for h in range(NA_HEADS):
        sl = slice(h * NA_HEAD_DIM, (h + 1) * NA_HEAD_DIM)
        q = q_ref[0, :, sl] * (NA_HEAD_DIM ** -0.5)
        s_b = lax.dot_general(q, kb_ref[0, :, sl], dn, preferred_element_type=F32) + bias_ref[0, h]
        s_c = lax.dot_general(q, kc_ref[0, :, sl], dn, preferred_element_type=F32)
        m = jnp.maximum(jnp.max(s_b, axis=-1, keepdims=True), jnp.max(s_c, axis=-1, keepdims=True))
        p_b = jnp.exp(s_b - m)
        p_c = jnp.exp(s_c - m)
        l = jnp.sum(p_b, axis=-1, keepdims=True) + jnp.sum(p_c, axis=-1, keepdims=True)
        o = (jnp.dot(p_b.astype(BF16), vb_ref[0, :, sl], preferred_element_type=F32)
             + jnp.dot(p_c.astype(BF16), vc_ref[0, :, sl], preferred_element_type=F32))
        o_ref[0, :, sl] = (o / l).astype(BF16)


def _na_bias_table(rpb):
    col = np.arange(GRID_W)
    col_start = np.clip(col - WIN_W // 2, 0, GRID_W - WIN_W)
    in_win = (col[None, :] >= col_start[:, None]) & (col[None, :] < col_start[:, None] + WIN_W)
    dc_idx = np.clip(col[None, :] - col[:, None], 1 - WIN_W, WIN_W - 1) + WIN_W - 1
    g = rpb.astype(F32)[:, :, dc_idx]
    g = jnp.where(jnp.asarray(in_win)[None, None], g, NEG_BIG)
    tabs = [g[:, off:off + WIN_H].transpose(0, 2, 1, 3).reshape(NA_HEADS, GRID_W, WIN_H * GRID_W)
            for off in range(WIN_H)]
    return jnp.stack(tabs)


def _na(qkv, qkv_c, bias_tab):
    b, s, _ = qkv.shape
    lc = qkv_c.shape[1]
    rows = s // GRID_W
    assert rows >= WIN_H
    nb = WIN_H * GRID_W

    def rs_of(r):
        return jnp.clip(r - WIN_H // 2, 0, rows - WIN_H)

    band = (pl.Element(1), pl.Element(nb), pl.Element(NA_WIDTH))

    return pl.pallas_call(
        functools.partial(_na_kernel, nb),
        grid=(b, rows),
        in_specs=[pl.BlockSpec((1, GRID_W, NA_WIDTH), lambda bi, r: (bi, r, 0)),
                  pl.BlockSpec(band, lambda bi, r: (bi, rs_of(r) * GRID_W, NA_WIDTH)),
                  pl.BlockSpec(band, lambda bi, r: (bi, rs_of(r) * GRID_W, 2 * NA_WIDTH)),
                  pl.BlockSpec((1, lc, NA_WIDTH), lambda bi, r: (bi, 0, 1)),
                  pl.BlockSpec((1, lc, NA_WIDTH), lambda bi, r: (bi, 0, 2)),
                  pl.BlockSpec((1, NA_HEADS, GRID_W, nb),
                               lambda bi, r: (rs_of(r) - r + WIN_H - 1, 0, 0, 0))],
        out_specs=pl.BlockSpec((1, GRID_W, NA_WIDTH), lambda bi, r: (bi, r, 0)),
        out_shape=jax.ShapeDtypeStruct((b, s, NA_WIDTH), BF16),
        compiler_params=_cparams(("parallel", "arbitrary")),
        name="na",
    )(qkv, qkv, qkv, qkv_c, qkv_c, bias_tab)


def _hg_constants(reverse):
    c = HG_CHUNK
    pos = np.arange(c)
    dmat = np.zeros((HG_LEVELS + 2, c, c), np.float32)
    qsel = np.zeros((HG_LEVELS, c, HG_DIM), np.float32)
    amask = np.zeros((HG_LEVELS + 1, c, c), np.float32)
    for l in range(HG_LEVELS):
        m = 1 << l
        blk = pos // (2 * m)
        bnd = blk * 2 * m + m - 1
        is_q = (pos % (2 * m)) >= m
        qsel[l] = is_q[:, None]
        for t in range(c):
            if is_q[t]:
                dmat[l, t, bnd[t] + 1:t + 1] = 1.0
            else:
                dmat[l, t, t + 1:bnd[t] + 1] = 1.0
        amask[l] = (blk[:, None] == blk[None, :]) & is_q[:, None] & ~is_q[None, :]
    for t in range(c):
        dmat[HG_LEVELS, t, :t + 1] = 1.0
        dmat[HG_LEVELS + 1, t, t + 1:] = 1.0
    amask[HG_LEVELS] = np.eye(c)
    if reverse:
        dmat = dmat[:, ::-1, ::-1]
        qsel = qsel[:, ::-1]
        amask = amask[:, ::-1, ::-1]
    return (jnp.asarray(dmat.reshape((HG_LEVELS + 2) * c, c), BF16),
            jnp.asarray(qsel, F32), jnp.asarray(amask, F32))


def _split3(a):
    hi = a.astype(BF16)
    r1 = a - hi.astype(F32)
    mid = r1.astype(BF16)
    lo = (r1 - mid.astype(F32)).astype(BF16)
    return hi, mid, lo


def _hg_block(q, z, v, lb, st, dmat, qsel_ref, amask_ref, order, reverse):
    c = HG_CHUNK
    dn_nt = (((1,), (1,)), ((), ()))
    f = lb + (1.0 - lb) * jax.nn.sigmoid(z)
    lf = jnp.log(f)
    kk = 1.0 - f
    qf = q.astype(F32)
    hi, mid, lo = _split3(lf)
    rows = lambda a, ci: a[ci * c:(ci + 1) * c]
    expo = {ci: (jnp.dot(dmat, rows(hi, ci), preferred_element_type=F32)
                 + jnp.dot(dmat, rows(mid, ci), preferred_element_type=F32)
                 + jnp.dot(dmat, rows(lo, ci), preferred_element_type=F32)) for ci in order}
    e = {ci: jnp.exp(expo[ci]) for ci in order}
    qsel = [qsel_ref[l] != 0.0 for l in range(HG_LEVELS)]
    amask = [amask_ref[l] != 0.0 for l in range(HG_LEVELS + 1)]
    a = {ci: lax.dot_general(rows(q, ci), rows(kk, ci).astype(BF16), dn_nt, preferred_element_type=F32)
         for ci in order}
    a = {ci: jnp.where(amask[HG_LEVELS], a[ci], 0.0) for ci in order}
    for l in range(HG_LEVELS):
        w = {ci: (e[ci][l * c:(l + 1) * c] * jnp.where(qsel[l], rows(qf, ci), rows(kk, ci))).astype(BF16)
             for ci in order}
        p = {ci: lax.dot_general(w[ci], w[ci], dn_nt, preferred_element_type=F32) for ci in order}
        a = {ci: jnp.where(amask[l], p[ci], a[ci]) for ci in order}
    e_in = {ci: e[ci][HG_LEVELS * c:(HG_LEVELS + 1) * c] for ci in order}
    e_out = {ci: e[ci][(HG_LEVELS + 1) * c:(HG_LEVELS + 2) * c] for ci in order}
    upd = {ci: lax.dot_general(rows(v, ci), (rows(kk, ci) * e_out[ci]).astype(BF16), (((0,), (0,)), ((), ())),
                               preferred_element_type=F32) for ci in order}
    o_intra = {ci: jnp.dot(a[ci].astype(BF16), rows(v, ci), preferred_element_type=F32) for ci in order}
    qd = {ci: (rows(qf, ci) * e_in[ci]).astype(BF16) for ci in order}
    out = {}
    for ci in order:
        out[ci] = o_intra[ci] + lax.dot_general(qd[ci], st.astype(BF16), dn_nt, preferred_element_type=F32)
        tot = e_in[ci][0:1] if reverse else e_in[ci][c - 1:c]
        st = st * tot + upd[ci]
    return out, st


def _hg_kernel(reverse, fuse_readout, nchunk, *refs):
    if fuse_readout:
        (q_ref, v_ref, z_ref, lb_ref, s0_ref, dmat_ref, qsel_ref, amask_ref, of_ref, og_ref, gn_ref,
         o_ref, sout_ref, st_scr) = refs
    else:
        (q_ref, v_ref, z_ref, lb_ref, s0_ref, dmat_ref, qsel_ref, amask_ref,
         o_ref, sout_ref, st_scr) = refs
    t = pl.program_id(2)

    @pl.when(t == 0)
    def _():
        st_scr[...] = s0_ref[0, 0]

    order = list(range(nchunk - 1, -1, -1) if reverse else range(nchunk))
    out, st = _hg_block(q_ref[0], z_ref[0], v_ref[0], lb_ref[0], st_scr[...], dmat_ref[...],
                        qsel_ref, amask_ref, order, reverse)
    o = jnp.concatenate([out[ci] for ci in range(nchunk)], axis=0)
    if fuse_readout:
        o = o + of_ref[0]
        on = o * lax.rsqrt(jnp.mean(o * o, axis=-1, keepdims=True) + EPS) * gn_ref[...]
        g = og_ref[0].astype(F32)
        o_ref[0] = (on * (g * jax.nn.sigmoid(g))).astype(o_ref.dtype)
    else:
        o_ref[0] = o
    st_scr[...] = st

    @pl.when(t == pl.num_programs(2) - 1)
    def _():
        sout_ref[0, 0] = st


def _hgrn(mix, fg, lb, s0, consts, reverse, d, o_fwd=None, gn=None):
    b, s, _ = mix.shape
    tt = _pick(s, (512, 256, 128, 64))
    nt = s // tt
    base = 2 * d // HG_DIM
    fuse = o_fwd is not None
    dmat, qsel, amask = consts
    fcol = HG_HEADS if reverse else 0

    def tmap(t):
        return nt - 1 - t if reverse else t

    blk = (1, tt, HG_DIM)
    in_specs = [pl.BlockSpec(blk, lambda bi, h, t: (bi, tmap(t), base + h)),
                pl.BlockSpec(blk, lambda bi, h, t: (bi, tmap(t), base + HG_HEADS + h)),
                pl.BlockSpec(blk, lambda bi, h, t: (bi, tmap(t), fcol + h)),
                pl.BlockSpec((1, 1, HG_DIM), lambda bi, h, t: (h, 0, 0)),
                pl.BlockSpec((1, 1, HG_DIM, HG_DIM), lambda bi, h, t: (bi, h, 0, 0)),
                pl.BlockSpec(dmat.shape, lambda bi, h, t: (0, 0)),
                pl.BlockSpec(qsel.shape, lambda bi, h, t: (0, 0, 0)),
                pl.BlockSpec(amask.shape, lambda bi, h, t: (0, 0, 0))]
    args = [mix, mix, fg, lb.reshape(HG_HEADS, 1, HG_DIM), s0, dmat, qsel, amask]
    if fuse:
        in_specs += [pl.BlockSpec(blk, lambda bi, h, t: (bi, tmap(t), h)),
                     pl.BlockSpec(blk, lambda bi, h, t: (bi, tmap(t), base + 2 * HG_HEADS + h)),
                     pl.BlockSpec((1, HG_DIM), lambda bi, h, t: (0, 0))]
        args += [o_fwd, mix, gn.reshape(1, HG_DIM)]
    return pl.pallas_call(
        functools.partial(_hg_kernel, reverse, fuse, tt // HG_CHUNK),
        grid=(b, HG_HEADS, nt),
        in_specs=in_specs,
        out_specs=[pl.BlockSpec(blk, lambda bi, h, t: (bi, tmap(t), h)),
                   pl.BlockSpec((1, 1, HG_DIM, HG_DIM), lambda bi, h, t: (bi, h, 0, 0))],
        out_shape=[jax.ShapeDtypeStruct((b, s, HG_WIDTH), BF16 if fuse else F32),
                   jax.ShapeDtypeStruct((b, HG_HEADS, HG_DIM, HG_DIM), F32)],
        scratch_shapes=[pltpu.VMEM((HG_DIM, HG_DIM), F32)],
        compiler_params=_cparams(("parallel", "parallel", "arbitrary")),
        name="hgrn_bwd" if reverse else "hgrn_fwd",
    )(*args)


def _merge_kernel(oa_ref, ob_ref, ga_ref, gb_ref, x_ref, g1_ref, n2_ref, sh2_ref, sc2_ref,
                  wpa_ref, wpb_ref, wo_ref, x1_ref, h2_ref):
    ya = jnp.dot(oa_ref[0], wpa_ref[...], preferred_element_type=F32)
    yb = jnp.dot(ob_ref[0], wpb_ref[...], preferred_element_type=F32)
    y = (jax.nn.sigmoid(ga_ref[0].astype(F32)) * ya + jax.nn.sigmoid(gb_ref[0].astype(F32)) * yb)
    z = jnp.dot(y.astype(BF16), wo_ref[...], preferred_element_type=F32)
    x1 = x_ref[0] + g1_ref[0] * z
    x1_ref[0] = x1
    ms = jnp.mean(x1 * x1, axis=-1, keepdims=True)
    hn = x1 * lax.rsqrt(ms + EPS) * n2_ref[...]
    h2_ref[0] = (hn * (1.0 + sc2_ref[0]) + sh2_ref[0]).astype(BF16)


def _merge(o_a, o_b, mix, x, g1, n2, sh2, sc2, wpa, wpb, wo):
    b, s, d = x.shape
    tm = _pick(s, (256, 128))
    const = lambda shape: pl.BlockSpec(shape, lambda bi, i: (0,) * len(shape), pipeline_mode=pl.Buffered(1))
    vec = pl.BlockSpec((1, 1, d), lambda bi, i: (bi, 0, 0))
    return pl.pallas_call(
        _merge_kernel,
        grid=(b, s // tm),
        in_specs=[pl.BlockSpec((1, tm, NA_WIDTH), lambda bi, i: (bi, i, 0)),
                  pl.BlockSpec((1, tm, HG_WIDTH), lambda bi, i: (bi, i, 0)),
                  pl.BlockSpec((1, tm, d), lambda bi, i: (bi, i, 0)),
                  pl.BlockSpec((1, tm, d), lambda bi, i: (bi, i, 1)),
                  pl.BlockSpec((1, tm, d), lambda bi, i: (bi, i, 0)),
                  vec, const((1, d)), vec, vec,
                  const(wpa.shape), const(wpb.shape), const(wo.shape)],
        out_specs=[pl.BlockSpec((1, tm, d), lambda bi, i: (bi, i, 0)),
                   pl.BlockSpec((1, tm, d), lambda bi, i: (bi, i, 0))],
        out_shape=[jax.ShapeDtypeStruct((b, s, d), F32), jax.ShapeDtypeStruct((b, s, d), BF16)],
        compiler_params=_cparams(("parallel", "parallel")),
        name="merge",
    )(o_a, o_b, mix, mix, x, g1, n2, sh2, sc2, wpa, wpb, wo)


def _ffn_kernel(h_ref, x1_ref, g2_ref, fg_ref, wa_ref, wu_ref, wo_ref, o_ref, acc):
    j = pl.program_id(2)

    @pl.when(j == 0)
    def _():
        acc[...] = jnp.zeros_like(acc)

    h = h_ref[0]
    a = jnp.dot(h, wa_ref[...], preferred_element_type=F32)
    u = jnp.dot(h, wu_ref[...], preferred_element_type=F32)
    gate = (a * jax.nn.sigmoid(a) * u).astype(BF16)
    acc[...] += jnp.dot(gate, wo_ref[...], preferred_element_type=F32)

    @pl.when(j == pl.num_programs(2) - 1)
    def _():
        x2 = x1_ref[0] + g2_ref[0] * acc[...]
        ms = jnp.mean(x2 * x2, axis=-1, keepdims=True)
        o_ref[0] = x2 * lax.rsqrt(ms + EPS) * fg_ref[...]


def _ffn(h2, x1, g2, final_g, w_in, w_out):
    b, s, d = x1.shape
    fh = w_out.shape[0]
    tf = _pick(fh, (512, 256, 128))
    nf = fh // tf
    tm = _pick(s, (512, 256, 128))
    return pl.pallas_call(
        _ffn_kernel,
        grid=(b, s // tm, nf),
        in_specs=[pl.BlockSpec((1, tm, d), lambda bi, i, j: (bi, i, 0)),
                  pl.BlockSpec((1, tm, d), lambda bi, i, j: (bi, i, 0)),
                  pl.BlockSpec((1, 1, d), lambda bi, i, j: (bi, 0, 0)),
                  pl.BlockSpec((1, d), lambda bi, i, j: (0, 0)),
                  pl.BlockSpec((d, tf), lambda bi, i, j: (0, j)),
                  pl.BlockSpec((d, tf), lambda bi, i, j: (0, nf + j)),
                  pl.BlockSpec((tf, d), lambda bi, i, j: (j, 0))],
        out_specs=pl.BlockSpec((1, tm, d), lambda bi, i, j: (bi, i, 0)),
        out_shape=jax.ShapeDtypeStruct((b, s, d), F32),
        scratch_shapes=[pltpu.VMEM((tm, d), F32)],
        compiler_params=_cparams(("parallel", "parallel", "arbitrary")),
        name="ffn",
    )(h2, x1, g2, final_g.reshape(1, d), w_in, w_in, w_out)


def kernel(x, c, ctx, c_ctx, w_ada, b_ada, norm1_g, w_in, na_rpb, hg_lb_logits, hg_norm_g,
           w_pa, w_pb, w_out, norm2_g, w_ffn_in, w_ffn_out, final_g):
    b, s, d = x.shape
    assert w_ada.shape[0] == 1, "single layer"

    lb_table = jnp.cumsum(jax.nn.softmax(hg_lb_logits.astype(F32), axis=0), axis=0)
    lb_f, lb_b = lb_table[0, 0], lb_table[0, 1]

    c8 = jnp.zeros((8, d), F32).at[:b].set(c).at[b].set(c_ctx)
    mod = _ada(c8, w_ada[0], b_ada[0]).reshape(8, N_MOD, d)
    sh1, sc1, g1, sh2, sc2, g2 = [mod[:b, j][:, None, :] for j in range(N_MOD)]
    sh1c = jnp.broadcast_to(mod[b, 0][None, None, :], (b, 1, d))
    sc1c = jnp.broadcast_to(mod[b, 1][None, None, :], (b, 1, d))

    o = np.cumsum((0, NA_WIDTH, NA_WIDTH, NA_WIDTH, HG_WIDTH, HG_WIDTH, HG_WIDTH, HG_WIDTH, HG_WIDTH, d, d))
    seg = lambda i: w_in[0][:, o[i]:o[i + 1]]
    w_perm = jnp.concatenate([seg(0), seg(1), seg(2), seg(8), seg(9), seg(3), seg(6), seg(7), seg(4), seg(5)],
                             axis=1).astype(BF16)
    n1 = norm1_g[0].reshape(1, d)

    qkv, mix, fg = _inproj(x, n1, sh1, sc1, w_perm, d)
    qkv_c, mix_c, fg_c = _inproj(ctx, n1, sh1c, sc1c, w_perm, d)

    o_na = _na(qkv, qkv_c, _na_bias_table(na_rpb[0]))

    cf, cb = _hg_constants(False), _hg_constants(True)
    zero = jnp.zeros((b, HG_HEADS, HG_DIM, HG_DIM), F32)
    _, s_f = _hgrn(mix_c, fg_c, lb_f, zero, cf, False, d)
    _, s_b = _hgrn(mix_c, fg_c, lb_b, zero, cb, True, d)
    o_f, _ = _hgrn(mix, fg, lb_f, s_f, cf, False, d)
    o_hg, _ = _hgrn(mix, fg, lb_b, s_b, cb, True, d, o_fwd=o_f, gn=hg_norm_g[0])

    x1, h2 = _merge(o_na, o_hg, mix, x, g1, norm2_g[0].reshape(1, d), sh2, sc2,
                    w_pa[0].astype(BF16), w_pb[0].astype(BF16), w_out[0].astype(BF16))
    return _ffn(h2, x1, g2, final_g, w_ffn_in[0].astype(BF16), w_ffn_out[0].astype(BF16))
```

```python
import functools

import jax
import jax.numpy as jnp
import numpy as np
from jax import lax
from jax.experimental import pallas as pl
from jax.experimental.pallas import tpu as pltpu

F32 = jnp.float32
BF16 = jnp.bfloat16

GRID_W = 64
WIN_H = 8
WIN_W = 16
NA_HEADS = 16
NA_HEAD_DIM = 64
NA_WIDTH = NA_HEADS * NA_HEAD_DIM
HG_HEADS = 8
HG_DIM = 128
HG_WIDTH = HG_HEADS * HG_DIM
N_MOD = 6
EPS = 1e-6
NEG_BIG = -1e30

V7X_VMEM_LIMIT_BYTES = 56 * 1024 * 1024
NA_PAIR_GROUP = 4
HG_CHUNK = 64
HG_LEVELS = 6


def _cparams(sem):
    return pltpu.CompilerParams(dimension_semantics=sem, vmem_limit_bytes=V7X_VMEM_LIMIT_BYTES)


def _pick(n, prefs):
    for p in prefs:
        if n % p == 0:
            return p
    return n


def _ada_kernel(c_ref, w_ref, b_ref, o_ref):
    c = c_ref[...]
    s = c * jax.nn.sigmoid(c)
    o_ref[...] = jnp.dot(s, w_ref[...], precision=lax.Precision.HIGHEST,
                         preferred_element_type=F32) + b_ref[...]


def _ada(c8, w, b):
    d, n = w.shape
    tn = _pick(n, (1024, 512, 256, 128))
    return pl.pallas_call(
        _ada_kernel,
        grid=(n // tn,),
        in_specs=[pl.BlockSpec((8, d), lambda j: (0, 0)),
                  pl.BlockSpec((d, tn), lambda j: (0, j)),
                  pl.BlockSpec((1, tn), lambda j: (0, j))],
        out_specs=pl.BlockSpec((8, tn), lambda j: (0, j)),
        out_shape=jax.ShapeDtypeStruct((8, n), F32),
        compiler_params=_cparams(("arbitrary",)),
        name="ada",
    )(c8, w, b.reshape(1, n))


def _inproj_kernel(n_qkv, n_mix, x_ref, g_ref, sh_ref, sc_ref, w_ref, qkv_ref, mix_ref, fg_ref, h_scr):
    j = pl.program_id(2)

    @pl.when(j == 0)
    def _():
        x = x_ref[0]
        ms = jnp.mean(x * x, axis=-1, keepdims=True)
        y = x * lax.rsqrt(ms + EPS) * g_ref[...]
        h_scr[...] = (y * (1.0 + sc_ref[0]) + sh_ref[0]).astype(BF16)

    u = jnp.dot(h_scr[...], w_ref[...], preferred_element_type=F32)

    @pl.when(j < n_qkv)
    def _():
        qkv_ref[0] = u.astype(BF16)

    @pl.when((j >= n_qkv) & (j < n_qkv + n_mix))
    def _():
        mix_ref[0] = u.astype(BF16)

    @pl.when(j >= n_qkv + n_mix)
    def _():
        fg_ref[0] = u


def _inproj(x, g, sh, sc, w, d):
    b, t, _ = x.shape
    w_qkv, w_mix, w_fg = 3 * NA_WIDTH, 2 * d + 3 * HG_WIDTH, 2 * HG_WIDTH
    tn = 512
    assert w_qkv % tn == 0 and w_mix % tn == 0 and w_fg % tn == 0
    n_qkv, n_mix, n_fg = w_qkv // tn, w_mix // tn, w_fg // tn
    tm = _pick(t, (1024, 512, 256))
    return pl.pallas_call(
        functools.partial(_inproj_kernel, n_qkv, n_mix),
        grid=(b, t // tm, n_qkv + n_mix + n_fg),
        in_specs=[pl.BlockSpec((1, tm, d), lambda bi, i, j: (bi, i, 0)),
                  pl.BlockSpec((1, d), lambda bi, i, j: (0, 0)),
                  pl.BlockSpec((1, 1, d), lambda bi, i, j: (bi, 0, 0)),
                  pl.BlockSpec((1, 1, d), lambda bi, i, j: (bi, 0, 0)),
                  pl.BlockSpec((d, tn), lambda bi, i, j: (0, j))],
        out_specs=[pl.BlockSpec((1, tm, tn), lambda bi, i, j: (bi, i, jnp.minimum(j, n_qkv - 1))),
                   pl.BlockSpec((1, tm, tn), lambda bi, i, j: (bi, i, jnp.clip(j - n_qkv, 0, n_mix - 1))),
                   pl.BlockSpec((1, tm, tn), lambda bi, i, j: (bi, i, jnp.maximum(j - n_qkv - n_mix, 0)))],
        out_shape=[jax.ShapeDtypeStruct((b, t, w_qkv), BF16),
                   jax.ShapeDtypeStruct((b, t, w_mix), BF16),
                   jax.ShapeDtypeStruct((b, t, w_fg), F32)],
        scratch_shapes=[pltpu.VMEM((tm, d), BF16)],
        compiler_params=_cparams(("parallel", "parallel", "arbitrary")),
        name="inproj",
    )(x, g, sh, sc, w)


def _na_kernel(nb, q_ref, kb_ref, vb_ref, kc_ref, vc_ref, bias_ref, o_ref):
    dn = (((1,), (1,)), ((), ()))
    first = lax.broadcasted_iota(jnp.int32, (GRID_W, 2 * NA_HEAD_DIM), 1) < NA_HEAD_DIM
    zero = jnp.zeros((GRID_W, 2 * NA_HEAD_DIM), BF16)
    for g0 in range(0, NA_HEADS // 2, NA_PAIR_GROUP):
        pairs = range(g0, g0 + NA_PAIR_GROUP)
        heads = [(p, a) for p in pairs for a in (0, 1)]
        cols = {p: slice(2 * p * NA_HEAD_DIM, 2 * (p + 1) * NA_HEAD_DIM) for p in pairs}
        q2 = {p: q_ref[0, :, cols[p]] * (NA_HEAD_DIM ** -0.5) for p in pairs}
        q = {(p, a): jnp.where(first, q2[p], zero) if a == 0 else jnp.where(first, zero, q2[p]) for p, a in heads}
        s_b = {(p, a): lax.dot_general(q[p, a], kb_ref[0, :, cols[p]], dn, preferred_element_type=F32)
               + bias_ref[0, 2 * p + a] for p, a in heads}
        s_c = {(p, a): lax.dot_general(q[p, a], kc_ref[0, :, cols[p]], dn, preferred_element_type=F32)
               for p, a in heads}
        m = {h: jnp.maximum(jnp.max(s_b[h], axis=-1, keepdims=True), jnp.max(s_c[h], axis=-1, keepdims=True))
             for h in heads}
        p_b = {h: jnp.exp(s_b[h] - m[h]) for h in heads}
        p_c = {h: jnp.exp(s_c[h] - m[h]) for h in heads}
        l = {h: jnp.sum(p_b[h], axis=-1, keepdims=True) + jnp.sum(p_c[h], axis=-1, keepdims=True) for h in heads}
        o = {(p, a): jnp.dot(p_b[p, a].astype(BF16), vb_ref[0, :, cols[p]], preferred_element_type=F32)
             + jnp.dot(p_c[p, a].astype(BF16), vc_ref[0, :, cols[p]], preferred_element_type=F32) for p, a in heads}
        for p in pairs:
            o_ref[0, :, cols[p]] = jnp.where(first, o[p, 0] / l[p, 0], o[p, 1] / l[p, 1]).astype(BF16)


def _na_bias_table(rpb):
    col = np.arange(GRID_W)
    col_start = np.clip(col - WIN_W // 2, 0, GRID_W - WIN_W)
    in_win = (col[None, :] >= col_start[:, None]) & (col[None, :] < col_start[:, None] + WIN_W)
    dc_idx = np.clip(col[None, :] - col[:, None], 1 - WIN_W, WIN_W - 1) + WIN_W - 1
    g = rpb.astype(F32)[:, :, dc_idx]
    g = jnp.where(jnp.asarray(in_win)[None, None], g, NEG_BIG)
    tabs = [g[:, off:off + WIN_H].transpose(0, 2, 1, 3).reshape(NA_HEADS, GRID_W, WIN_H * GRID_W)
            for off in range(WIN_H)]
    return jnp.stack(tabs)


def _na(qkv, qkv_c, bias_tab):
    b, s, _ = qkv.shape
    lc = qkv_c.shape[1]
    rows = s // GRID_W
    assert rows >= WIN_H
    nb = WIN_H * GRID_W

    def rs_of(r):
        return jnp.clip(r - WIN_H // 2, 0, rows - WIN_H)

    band = (pl.Element(1), pl.Element(nb), pl.Element(NA_WIDTH))

    return pl.pallas_call(
        functools.partial(_na_kernel, nb),
        grid=(b, rows),
        in_specs=[pl.BlockSpec((1, GRID_W, NA_WIDTH), lambda bi, r: (bi, r, 0)),
                  pl.BlockSpec(band, lambda bi, r: (bi, rs_of(r) * GRID_W, NA_WIDTH)),
                  pl.BlockSpec(band, lambda bi, r: (bi, rs_of(r) * GRID_W, 2 * NA_WIDTH)),
                  pl.BlockSpec((1, lc, NA_WIDTH), lambda bi, r: (bi, 0, 1)),
                  pl.BlockSpec((1, lc, NA_WIDTH), lambda bi, r: (bi, 0, 2)),
                  pl.BlockSpec((1, NA_HEADS, GRID_W, nb),
                               lambda bi, r: (rs_of(r) - r + WIN_H - 1, 0, 0, 0))],
        out_specs=pl.BlockSpec((1, GRID_W, NA_WIDTH), lambda bi, r: (bi, r, 0)),
        out_shape=jax.ShapeDtypeStruct((b, s, NA_WIDTH), BF16),
        compiler_params=_cparams(("parallel", "arbitrary")),
        name="na",
    )(qkv, qkv, qkv, qkv_c, qkv_c, bias_tab)


def _hg_constants(reverse):
    c = HG_CHUNK
    pos = np.arange(c)
    dmat = np.zeros((HG_LEVELS + 2, c, c), np.float32)
    qsel = np.zeros((HG_LEVELS, c, HG_DIM), np.float32)
    amask = np.zeros((HG_LEVELS + 1, c, c), np.float32)
    for l in range(HG_LEVELS):
        m = 1 << l
        blk = pos // (2 * m)
        bnd = blk * 2 * m + m - 1
        is_q = (pos % (2 * m)) >= m
        qsel[l] = is_q[:, None]
        for t in range(c):
            if is_q[t]:
                dmat[l, t, bnd[t] + 1:t + 1] = 1.0
            else:
                dmat[l, t, t + 1:bnd[t] + 1] = 1.0
        amask[l] = (blk[:, None] == blk[None, :]) & is_q[:, None] & ~is_q[None, :]
    for t in range(c):
        dmat[HG_LEVELS, t, :t + 1] = 1.0
        dmat[HG_LEVELS + 1, t, t + 1:] = 1.0
    amask[HG_LEVELS] = np.eye(c)
    if reverse:
        dmat = dmat[:, ::-1, ::-1]
        qsel = qsel[:, ::-1]
        amask = amask[:, ::-1, ::-1]
    return (jnp.asarray(dmat.reshape((HG_LEVELS + 2) * c, c), BF16),
            jnp.asarray(qsel, F32), jnp.asarray(amask, F32))


def _split3(a):
    hi = a.astype(BF16)
    r1 = a - hi.astype(F32)
    mid = r1.astype(BF16)
    lo = (r1 - mid.astype(F32)).astype(BF16)
    return hi, mid, lo


def _hg_block(q, z, v, lb, st, dmat, qsel_ref, amask_ref, order, reverse):
    c = HG_CHUNK
    dn_nt = (((1,), (1,)), ((), ()))
    f = lb + (1.0 - lb) * jax.nn.sigmoid(z)
    lf = jnp.log(f)
    kk = 1.0 - f
    qf = q.astype(F32)
    hi, mid, lo = _split3(lf)
    rows = lambda a, ci: a[ci * c:(ci + 1) * c]
    expo = {ci: (jnp.dot(dmat, rows(hi, ci), preferred_element_type=F32)
                 + jnp.dot(dmat, rows(mid, ci), preferred_element_type=F32)
                 + jnp.dot(dmat, rows(lo, ci), preferred_element_type=F32)) for ci in order}
    e = {ci: jnp.exp(expo[ci]) for ci in order}
    qsel = [qsel_ref[l] != 0.0 for l in range(HG_LEVELS)]
    amask = [amask_ref[l] != 0.0 for l in range(HG_LEVELS + 1)]
    a = {ci: lax.dot_general(rows(q, ci), rows(kk, ci).astype(BF16), dn_nt, preferred_element_type=F32)
         for ci in order}
    a = {ci: jnp.where(amask[HG_LEVELS], a[ci], 0.0) for ci in order}
    for l in range(HG_LEVELS):
        w = {ci: (e[ci][l * c:(l + 1) * c] * jnp.where(qsel[l], rows(qf, ci), rows(kk, ci))).astype(BF16)
             for ci in order}
        p = {ci: lax.dot_general(w[ci], w[ci], dn_nt, preferred_element_type=F32) for ci in order}
        a = {ci: jnp.where(amask[l], p[ci], a[ci]) for ci in order}
    e_in = {ci: e[ci][HG_LEVELS * c:(HG_LEVELS + 1) * c] for ci in order}
    e_out = {ci: e[ci][(HG_LEVELS + 1) * c:(HG_LEVELS + 2) * c] for ci in order}
    upd = {ci: lax.dot_general(rows(v, ci), (rows(kk, ci) * e_out[ci]).astype(BF16), (((0,), (0,)), ((), ())),
                               preferred_element_type=F32) for ci in order}
    o_intra = {ci: jnp.dot(a[ci].astype(BF16), rows(v, ci), preferred_element_type=F32) for ci in order}
    qd = {ci: (rows(qf, ci) * e_in[ci]).astype(BF16) for ci in order}
    out = {}
    for ci in order:
        out[ci] = o_intra[ci] + lax.dot_general(qd[ci], st.astype(BF16), dn_nt, preferred_element_type=F32)
        tot = e_in[ci][0:1] if reverse else e_in[ci][c - 1:c]
        st = st * tot + upd[ci]
    return out, st


def _hg_kernel(reverse, fuse_readout, nchunk, *refs):
    if fuse_readout:
        (q_ref, v_ref, z_ref, lb_ref, s0_ref, dmat_ref, qsel_ref, amask_ref, of_ref, og_ref, gn_ref,
         o_ref, sout_ref, st_scr) = refs
    else:
        (q_ref, v_ref, z_ref, lb_ref, s0_ref, dmat_ref, qsel_ref, amask_ref,
         o_ref, sout_ref, st_scr) = refs
    t = pl.program_id(2)

    @pl.when(t == 0)
    def _():
        st_scr[...] = s0_ref[0, 0]

    order = list(range(nchunk - 1, -1, -1) if reverse else range(nchunk))
    out, st = _hg_block(q_ref[0], z_ref[0], v_ref[0], lb_ref[0], st_scr[...], dmat_ref[...],
                        qsel_ref, amask_ref, order, reverse)
    o = jnp.concatenate([out[ci] for ci in range(nchunk)], axis=0)
    if fuse_readout:
        o = o + of_ref[0]
        on = o * lax.rsqrt(jnp.mean(o * o, axis=-1, keepdims=True) + EPS) * gn_ref[...]
        g = og_ref[0].astype(F32)
        o_ref[0] = (on * (g * jax.nn.sigmoid(g))).astype(o_ref.dtype)
    else:
        o_ref[0] = o
    st_scr[...] = st

    @pl.when(t == pl.num_programs(2) - 1)
    def _():
        sout_ref[0, 0] = st


def _hgrn(mix, fg, lb, s0, consts, reverse, d, o_fwd=None, gn=None):
    b, s, _ = mix.shape
    tt = _pick(s, (512, 256, 128, 64))
    nt = s // tt
    base = 2 * d // HG_DIM
    fuse = o_fwd is not None
    dmat, qsel, amask = consts
    fcol = HG_HEADS if reverse else 0

    def tmap(t):
        return nt - 1 - t if reverse else t

    blk = (1, tt, HG_DIM)
    in_specs = [pl.BlockSpec(blk, lambda bi, h, t: (bi, tmap(t), base + h)),
                pl.BlockSpec(blk, lambda bi, h, t: (bi, tmap(t), base + HG_HEADS + h)),
                pl.BlockSpec(blk, lambda bi, h, t: (bi, tmap(t), fcol + h)),
                pl.BlockSpec((1, 1, HG_DIM), lambda bi, h, t: (h, 0, 0)),
                pl.BlockSpec((1, 1, HG_DIM, HG_DIM), lambda bi, h, t: (bi, h, 0, 0)),
                pl.BlockSpec(dmat.shape, lambda bi, h, t: (0, 0)),
                pl.BlockSpec(qsel.shape, lambda bi, h, t: (0, 0, 0)),
                pl.BlockSpec(amask.shape, lambda bi, h, t: (0, 0, 0))]
    args = [mix, mix, fg, lb.reshape(HG_HEADS, 1, HG_DIM), s0, dmat, qsel, amask]
    if fuse:
        in_specs += [pl.BlockSpec(blk, lambda bi, h, t: (bi, tmap(t), h)),
                     pl.BlockSpec(blk, lambda bi, h, t: (bi, tmap(t), base + 2 * HG_HEADS + h)),
                     pl.BlockSpec((1, HG_DIM), lambda bi, h, t: (0, 0))]
        args += [o_fwd, mix, gn.reshape(1, HG_DIM)]
    return pl.pallas_call(
        functools.partial(_hg_kernel, reverse, fuse, tt // HG_CHUNK),
        grid=(b, HG_HEADS, nt),
        in_specs=in_specs,
        out_specs=[pl.BlockSpec(blk, lambda bi, h, t: (bi, tmap(t), h)),
                   pl.BlockSpec((1, 1, HG_DIM, HG_DIM), lambda bi, h, t: (bi, h, 0, 0))],
        out_shape=[jax.ShapeDtypeStruct((b, s, HG_WIDTH), BF16 if fuse else F32),
                   jax.ShapeDtypeStruct((b, HG_HEADS, HG_DIM, HG_DIM), F32)],
        scratch_shapes=[pltpu.VMEM((HG_DIM, HG_DIM), F32)],
        compiler_params=_cparams(("parallel", "parallel", "arbitrary")),
        name="hgrn_bwd" if reverse else "hgrn_fwd",
    )(*args)


def _merge_kernel(oa_ref, ob_ref, ga_ref, gb_ref, x_ref, g1_ref, n2_ref, sh2_ref, sc2_ref,
                  wpa_ref, wpb_ref, wo_ref, x1_ref, h2_ref):
    ya = jnp.dot(oa_ref[0], wpa_ref[...], preferred_element_type=F32)
    yb = jnp.dot(ob_ref[0], wpb_ref[...], preferred_element_type=F32)
    y = (jax.nn.sigmoid(ga_ref[0].astype(F32)) * ya + jax.nn.sigmoid(gb_ref[0].astype(F32)) * yb)
    z = jnp.dot(y.astype(BF16), wo_ref[...], preferred_element_type=F32)
    x1 = x_ref[0] + g1_ref[0] * z
    x1_ref[0] = x1
    ms = jnp.mean(x1 * x1, axis=-1, keepdims=True)
    hn = x1 * lax.rsqrt(ms + EPS) * n2_ref[...]
    h2_ref[0] = (hn * (1.0 + sc2_ref[0]) + sh2_ref[0]).astype(BF16)


def _merge(o_a, o_b, mix, x, g1, n2, sh2, sc2, wpa, wpb, wo):
    b, s, d = x.shape
    tm = _pick(s, (256, 128))
    const = lambda shape: pl.BlockSpec(shape, lambda bi, i: (0,) * len(shape), pipeline_mode=pl.Buffered(1))
    vec = pl.BlockSpec((1, 1, d), lambda bi, i: (bi, 0, 0))
    return pl.pallas_call(
        _merge_kernel,
        grid=(b, s // tm),
        in_specs=[pl.BlockSpec((1, tm, NA_WIDTH), lambda bi, i: (bi, i, 0)),
                  pl.BlockSpec((1, tm, HG_WIDTH), lambda bi, i: (bi, i, 0)),
                  pl.BlockSpec((1, tm, d), lambda bi, i: (bi, i, 0)),
                  pl.BlockSpec((1, tm, d), lambda bi, i: (bi, i, 1)),
                  pl.BlockSpec((1, tm, d), lambda bi, i: (bi, i, 0)),
                  vec, const((1, d)), vec, vec,
                  const(wpa.shape), const(wpb.shape), const(wo.shape)],
        out_specs=[pl.BlockSpec((1, tm, d), lambda bi, i: (bi, i, 0)),
                   pl.BlockSpec((1, tm, d), lambda bi, i: (bi, i, 0))],
        out_shape=[jax.ShapeDtypeStruct((b, s, d), F32), jax.ShapeDtypeStruct((b, s, d), BF16)],
        compiler_params=_cparams(("parallel", "parallel")),
        name="merge",
    )(o_a, o_b, mix, mix, x, g1, n2, sh2, sc2, wpa, wpb, wo)


def _ffn_kernel(h_ref, x1_ref, g2_ref, fg_ref, wa_ref, wu_ref, wo_ref, o_ref):
    j = pl.program_id(2)
    h = h_ref[0]
    a = jnp.dot(h, wa_ref[...], preferred_element_type=F32)
    u = jnp.dot(h, wu_ref[...], preferred_element_type=F32)
    gate = (a * jax.nn.sigmoid(a) * u).astype(BF16)

    @pl.when(j == 0)
    def _():
        o_ref[0] = jnp.zeros(o_ref.shape[1:], F32)

    d = o_ref.shape[2]
    nc = _pick(d, (512, 256, 128))
    for n0 in range(0, d, nc):
        o_ref[0, :, n0:n0 + nc] += jnp.dot(gate, wo_ref[:, n0:n0 + nc], preferred_element_type=F32)

    @pl.when(j == pl.num_programs(2) - 1)
    def _():
        x2 = x1_ref[0] + g2_ref[0] * o_ref[0]
        ms = jnp.mean(x2 * x2, axis=-1, keepdims=True)
        o_ref[0] = x2 * lax.rsqrt(ms + EPS) * fg_ref[...]


def _ffn(h2, x1, g2, final_g, w_in, w_out):
    b, s, d = x1.shape
    fh = w_out.shape[0]
    tf = _pick(fh, (256, 128))
    nf = fh // tf
    tm = _pick(s, (1024, 512, 256, 128))
    return pl.pallas_call(
        _ffn_kernel,
        grid=(b, s // tm, nf),
        in_specs=[pl.BlockSpec((1, tm, d), lambda bi, i, j: (bi, i, 0)),
                  pl.BlockSpec((1, tm, d), lambda bi, i, j: (bi, i, 0)),
                  pl.BlockSpec((1, 1, d), lambda bi, i, j: (bi, 0, 0)),
                  pl.BlockSpec((1, d), lambda bi, i, j: (0, 0)),
                  pl.BlockSpec((d, tf), lambda bi, i, j: (0, j)),
                  pl.BlockSpec((d, tf), lambda bi, i, j: (0, nf + j)),
                  pl.BlockSpec((tf, d), lambda bi, i, j: (j, 0))],
        out_specs=pl.BlockSpec((1, tm, d), lambda bi, i, j: (bi, i, 0)),
        out_shape=jax.ShapeDtypeStruct((b, s, d), F32),
        compiler_params=_cparams(("parallel", "parallel", "arbitrary")),
        name="ffn",
    )(h2, x1, g2, final_g.reshape(1, d), w_in, w_in, w_out)


def kernel(x, c, ctx, c_ctx, w_ada, b_ada, norm1_g, w_in, na_rpb, hg_lb_logits, hg_norm_g,
           w_pa, w_pb, w_out, norm2_g, w_ffn_in, w_ffn_out, final_g):
    b, s, d = x.shape
    assert w_ada.shape[0] == 1, "single layer"

    lb_table = jnp.cumsum(jax.nn.softmax(hg_lb_logits.astype(F32), axis=0), axis=0)
    lb_f, lb_b = lb_table[0, 0], lb_table[0, 1]

    c8 = jnp.zeros((8, d), F32).at[:b].set(c).at[b].set(c_ctx)
    mod = _ada(c8, w_ada[0], b_ada[0]).reshape(8, N_MOD, d)
    sh1, sc1, g1, sh2, sc2, g2 = [mod[:b, j][:, None, :] for j in range(N_MOD)]
    sh1c = jnp.broadcast_to(mod[b, 0][None, None, :], (b, 1, d))
    sc1c = jnp.broadcast_to(mod[b, 1][None, None, :], (b, 1, d))

    o = np.cumsum((0, NA_WIDTH, NA_WIDTH, NA_WIDTH, HG_WIDTH, HG_WIDTH, HG_WIDTH, HG_WIDTH, HG_WIDTH, d, d))
    seg = lambda i: w_in[0][:, o[i]:o[i + 1]]
    w_perm = jnp.concatenate([seg(0), seg(1), seg(2), seg(8), seg(9), seg(3), seg(6), seg(7), seg(4), seg(5)],
                             axis=1).astype(BF16)
    n1 = norm1_g[0].reshape(1, d)

    qkv, mix, fg = _inproj(x, n1, sh1, sc1, w_perm, d)
    qkv_c, mix_c, fg_c = _inproj(ctx, n1, sh1c, sc1c, w_perm, d)

    o_na = _na(qkv, qkv_c, _na_bias_table(na_rpb[0]))

    cf, cb = _hg_constants(False), _hg_constants(True)
    zero = jnp.zeros((b, HG_HEADS, HG_DIM, HG_DIM), F32)
    _, s_f = _hgrn(mix_c, fg_c, lb_f, zero, cf, False, d)
    _, s_b = _hgrn(mix_c, fg_c, lb_b, zero, cb, True, d)
    o_f, _ = _hgrn(mix, fg, lb_f, s_f, cf, False, d)
    o_hg, _ = _hgrn(mix, fg, lb_b, s_b, cb, True, d, o_fwd=o_f, gn=hg_norm_g[0])

    x1, h2 = _merge(o_na, o_hg, mix, x, g1, norm2_g[0].reshape(1, d), sh2, sc2,
                    w_pa[0].astype(BF16), w_pb[0].astype(BF16), w_out[0].astype(BF16))
    return _ffn(h2, x1, g2, final_g, w_ffn_in[0].astype(BF16), w_ffn_out[0].astype(BF16))
```

```python
import functools

import jax
import jax.numpy as jnp
import numpy as np
from jax import lax
from jax.experimental import pallas as pl
from jax.experimental.pallas import tpu as pltpu

F32 = jnp.float32
BF16 = jnp.bfloat16

GRID_W = 64
WIN_H = 8
WIN_W = 16
NA_HEADS = 16
NA_HEAD_DIM = 64
NA_WIDTH = NA_HEADS * NA_HEAD_DIM
HG_HEADS = 8
HG_DIM = 128
HG_WIDTH = HG_HEADS * HG_DIM
N_MOD = 6
EPS = 1e-6
NEG_BIG = -1e30

V7X_VMEM_LIMIT_BYTES = 60000 * 1024
NA_PAIR_GROUP = 4
HG_CHUNK = 64
HG_LEVELS = 6


def _cparams(sem):
    return pltpu.CompilerParams(dimension_semantics=sem, vmem_limit_bytes=V7X_VMEM_LIMIT_BYTES)


def _pick(n, prefs):
    for p in prefs:
        if n % p == 0:
            return p
    return n


def _ada_kernel(c_ref, w_ref, b_ref, o_ref):
    c = c_ref[...]
    s = c * jax.nn.sigmoid(c)
    o_ref[...] = jnp.dot(s, w_ref[...], precision=lax.Precision.HIGHEST,
                         preferred_element_type=F32) + b_ref[...]


def _ada(c8, w, b):
    d, n = w.shape
    tn = _pick(n, (1024, 512, 256, 128))
    return pl.pallas_call(
        _ada_kernel,
        grid=(n // tn,),
        in_specs=[pl.BlockSpec((8, d), lambda j: (0, 0)),
                  pl.BlockSpec((d, tn), lambda j: (0, j)),
                  pl.BlockSpec((1, tn), lambda j: (0, j))],
        out_specs=pl.BlockSpec((8, tn), lambda j: (0, j)),
        out_shape=jax.ShapeDtypeStruct((8, n), F32),
        compiler_params=_cparams(("arbitrary",)),
        name="ada",
    )(c8, w, b.reshape(1, n))


def _prenorm_kernel(x_ref, g_ref, sh_ref, sc_ref, h_ref):
    x = x_ref[0]
    ms = jnp.mean(x * x, axis=-1, keepdims=True)
    y = x * lax.rsqrt(ms + EPS) * g_ref[...]
    h_ref[0] = (y * (1.0 + sc_ref[0]) + sh_ref[0]).astype(BF16)


def _prenorm(x, g, sh, sc):
    b, t, d = x.shape
    tm = _pick(t, (512, 256))
    vec = pl.BlockSpec((1, 1, d), lambda bi, i: (bi, 0, 0))
    return pl.pallas_call(
        _prenorm_kernel,
        grid=(b, t // tm),
        in_specs=[pl.BlockSpec((1, tm, d), lambda bi, i: (bi, i, 0)),
                  pl.BlockSpec((1, d), lambda bi, i: (0, 0)), vec, vec],
        out_specs=pl.BlockSpec((1, tm, d), lambda bi, i: (bi, i, 0)),
        out_shape=jax.ShapeDtypeStruct((b, t, d), BF16),
        compiler_params=_cparams(("parallel", "parallel")),
        name="prenorm",
    )(x, g, sh, sc)


def _inproj_kernel(n_qkv, n_mix, h_ref, w_ref, qkv_ref, mix_ref, fg_ref):
    j = pl.program_id(2)
    u = jnp.dot(h_ref[0], w_ref[...], preferred_element_type=F32)

    @pl.when(j < n_qkv)
    def _():
        qkv_ref[0] = u.astype(BF16)

    @pl.when((j >= n_qkv) & (j < n_qkv + n_mix))
    def _():
        mix_ref[0] = u.astype(BF16)

    @pl.when(j >= n_qkv + n_mix)
    def _():
        fg_ref[0] = u


def _inproj(h, w):
    b, t, d = h.shape
    w_qkv, w_mix, w_fg = 3 * NA_WIDTH, 2 * d + 3 * HG_WIDTH, 2 * HG_WIDTH
    tn = 1024
    assert w_qkv % tn == 0 and w_mix % tn == 0 and w_fg % tn == 0
    n_qkv, n_mix, n_fg = w_qkv // tn, w_mix // tn, w_fg // tn
    tm = _pick(t, (1024, 512, 256))
    return pl.pallas_call(
        functools.partial(_inproj_kernel, n_qkv, n_mix),
        grid=(b, t // tm, n_qkv + n_mix + n_fg),
        in_specs=[pl.BlockSpec((1, tm, d), lambda bi, i, j: (bi, i, 0)),
                  pl.BlockSpec((d, tn), lambda bi, i, j: (0, j))],
        out_specs=[pl.BlockSpec((1, tm, tn), lambda bi, i, j: (bi, i, jnp.minimum(j, n_qkv - 1))),
                   pl.BlockSpec((1, tm, tn), lambda bi, i, j: (bi, i, jnp.clip(j - n_qkv, 0, n_mix - 1))),
                   pl.BlockSpec((1, tm, tn), lambda bi, i, j: (bi, i, jnp.maximum(j - n_qkv - n_mix, 0)))],
        out_shape=[jax.ShapeDtypeStruct((b, t, w_qkv), BF16),
                   jax.ShapeDtypeStruct((b, t, w_mix), BF16),
                   jax.ShapeDtypeStruct((b, t, w_fg), F32)],
        compiler_params=_cparams(("parallel", "parallel", "arbitrary")),
        name="inproj",
    )(h, w)


def _na_kernel(nb, q_ref, kb_ref, vb_ref, kc_ref, vc_ref, bias_ref, o_ref):
    dn = (((1,), (1,)), ((), ()))
    first = lax.broadcasted_iota(jnp.int32, (GRID_W, 2 * NA_HEAD_DIM), 1) < NA_HEAD_DIM
    zero = jnp.zeros((GRID_W, 2 * NA_HEAD_DIM), BF16)
    for g0 in range(0, NA_HEADS // 2, NA_PAIR_GROUP):
        pairs = range(g0, g0 + NA_PAIR_GROUP)
        heads = [(p, a) for p in pairs for a in (0, 1)]
        cols = {p: slice(2 * p * NA_HEAD_DIM, 2 * (p + 1) * NA_HEAD_DIM) for p in pairs}
        q2 = {p: q_ref[0, :, cols[p]] * (NA_HEAD_DIM ** -0.5) for p in pairs}
        q = {(p, a): jnp.where(first, q2[p], zero) if a == 0 else jnp.where(first, zero, q2[p]) for p, a in heads}
        s_b = {(p, a): lax.dot_general(q[p, a], kb_ref[0, :, cols[p]], dn, preferred_element_type=F32)
               + bias_ref[0, 2 * p + a] for p, a in heads}
        s_c = {(p, a): lax.dot_general(q[p, a], kc_ref[0, :, cols[p]], dn, preferred_element_type=F32)
               for p, a in heads}
        m = {h: jnp.maximum(jnp.max(s_b[h], axis=-1, keepdims=True), jnp.max(s_c[h], axis=-1, keepdims=True))
             for h in heads}
        p_b = {h: jnp.exp(s_b[h] - m[h]) for h in heads}
        p_c = {h: jnp.exp(s_c[h] - m[h]) for h in heads}
        l = {h: jnp.sum(p_b[h], axis=-1, keepdims=True) + jnp.sum(p_c[h], axis=-1, keepdims=True) for h in heads}
        o = {(p, a): jnp.dot(p_b[p, a].astype(BF16), vb_ref[0, :, cols[p]], preferred_element_type=F32)
             + jnp.dot(p_c[p, a].astype(BF16), vc_ref[0, :, cols[p]], preferred_element_type=F32) for p, a in heads}
        for p in pairs:
            o_ref[0, :, cols[p]] = jnp.where(first, o[p, 0] / l[p, 0], o[p, 1] / l[p, 1]).astype(BF16)


def _na_bias_table(rpb):
    col = np.arange(GRID_W)
    col_start = np.clip(col - WIN_W // 2, 0, GRID_W - WIN_W)
    in_win = (col[None, :] >= col_start[:, None]) & (col[None, :] < col_start[:, None] + WIN_W)
    dc_idx = np.clip(col[None, :] - col[:, None], 1 - WIN_W, WIN_W - 1) + WIN_W - 1
    g = rpb.astype(F32)[:, :, dc_idx]
    g = jnp.where(jnp.asarray(in_win)[None, None], g, NEG_BIG)
    tabs = [g[:, off:off + WIN_H].transpose(0, 2, 1, 3).reshape(NA_HEADS, GRID_W, WIN_H * GRID_W)
            for off in range(WIN_H)]
    return jnp.stack(tabs)


def _na(qkv, qkv_c, bias_tab):
    b, s, _ = qkv.shape
    lc = qkv_c.shape[1]
    rows = s // GRID_W
    assert rows >= WIN_H
    nb = WIN_H * GRID_W

    def rs_of(r):
        return jnp.clip(r - WIN_H // 2, 0, rows - WIN_H)

    band = (pl.Element(1), pl.Element(nb), pl.Element(NA_WIDTH))

    return pl.pallas_call(
        functools.partial(_na_kernel, nb),
        grid=(b, rows),
        in_specs=[pl.BlockSpec((1, GRID_W, NA_WIDTH), lambda bi, r: (bi, r, 0)),
                  pl.BlockSpec(band, lambda bi, r: (bi, rs_of(r) * GRID_W, NA_WIDTH)),
                  pl.BlockSpec(band, lambda bi, r: (bi, rs_of(r) * GRID_W, 2 * NA_WIDTH)),
                  pl.BlockSpec((1, lc, NA_WIDTH), lambda bi, r: (bi, 0, 1)),
                  pl.BlockSpec((1, lc, NA_WIDTH), lambda bi, r: (bi, 0, 2)),
                  pl.BlockSpec((1, NA_HEADS, GRID_W, nb),
                               lambda bi, r: (rs_of(r) - r + WIN_H - 1, 0, 0, 0))],
        out_specs=pl.BlockSpec((1, GRID_W, NA_WIDTH), lambda bi, r: (bi, r, 0)),
        out_shape=jax.ShapeDtypeStruct((b, s, NA_WIDTH), BF16),
        compiler_params=_cparams(("parallel", "arbitrary")),
        name="na",
    )(qkv, qkv, qkv, qkv_c, qkv_c, bias_tab)


def _hg_constants(reverse):
    c = HG_CHUNK
    pos = np.arange(c)
    dmat = np.zeros((HG_LEVELS + 2, c, c), np.float32)
    qsel = np.zeros((HG_LEVELS, c, HG_DIM), np.float32)
    amask = np.zeros((HG_LEVELS + 1, c, c), np.float32)
    for l in range(HG_LEVELS):
        m = 1 << l
        blk = pos // (2 * m)
        bnd = blk * 2 * m + m - 1
        is_q = (pos % (2 * m)) >= m
        qsel[l] = is_q[:, None]
        for t in range(c):
            if is_q[t]:
                dmat[l, t, bnd[t] + 1:t + 1] = 1.0
            else:
                dmat[l, t, t + 1:bnd[t] + 1] = 1.0
        amask[l] = (blk[:, None] == blk[None, :]) & is_q[:, None] & ~is_q[None, :]
    for t in range(c):
        dmat[HG_LEVELS, t, :t + 1] = 1.0
        dmat[HG_LEVELS + 1, t, t + 1:] = 1.0
    amask[HG_LEVELS] = np.eye(c)
    if reverse:
        dmat = dmat[:, ::-1, ::-1]
        qsel = qsel[:, ::-1]
        amask = amask[:, ::-1, ::-1]
    dmat = dmat.reshape((HG_LEVELS + 2) * c, c)
    return (jnp.asarray(np.concatenate([dmat, dmat, dmat], axis=1), BF16),
            jnp.asarray(qsel, F32), jnp.asarray(amask, F32))


def _split3(a):
    hi = a.astype(BF16)
    r1 = a - hi.astype(F32)
    mid = r1.astype(BF16)
    lo = (r1 - mid.astype(F32)).astype(BF16)
    return hi, mid, lo


def _hg_block(q, z, v, lb, st, dmat, qsel_ref, amask_ref, order, reverse):
    c = HG_CHUNK
    dn_nt = (((1,), (1,)), ((), ()))
    f = lb + (1.0 - lb) * jax.nn.sigmoid(z)
    lf = jnp.log(f)
    kk = 1.0 - f
    qf = q.astype(F32)
    hi, mid, lo = _split3(lf)
    rows = lambda a, ci: a[ci * c:(ci + 1) * c]
    expo = {}
    for ca, cb in zip(order[0::2], order[1::2]):
        pieces = jnp.concatenate(
            [jnp.concatenate([rows(a, ca), rows(a, cb)], axis=1) for a in (hi, mid, lo)], axis=0)
        both = jnp.dot(dmat, pieces, preferred_element_type=F32)
        expo[ca], expo[cb] = both[:, :HG_DIM], both[:, HG_DIM:]
    if len(order) % 2:
        ci = order[-1]
        expo[ci] = jnp.dot(dmat, jnp.concatenate([rows(a, ci) for a in (hi, mid, lo)], axis=0),
                           preferred_element_type=F32)
    e = {ci: jnp.exp(expo[ci]) for ci in order}
    qsel = [qsel_ref[l] != 0.0 for l in range(HG_LEVELS)]
    amask = [amask_ref[l] != 0.0 for l in range(HG_LEVELS + 1)]
    a = {ci: lax.dot_general(rows(q, ci), rows(kk, ci).astype(BF16), dn_nt, preferred_element_type=F32)
         for ci in order}
    a = {ci: jnp.where(amask[HG_LEVELS], a[ci], 0.0) for ci in order}
    for l in range(HG_LEVELS):
        w = {ci: (e[ci][l * c:(l + 1) * c] * jnp.where(qsel[l], rows(qf, ci), rows(kk, ci))).astype(BF16)
             for ci in order}
        p = {ci: lax.dot_general(w[ci], w[ci], dn_nt, preferred_element_type=F32) for ci in order}
        a = {ci: jnp.where(amask[l], p[ci], a[ci]) for ci in order}
    e_in = {ci: e[ci][HG_LEVELS * c:(HG_LEVELS + 1) * c] for ci in order}
    e_out = {ci: e[ci][(HG_LEVELS + 1) * c:(HG_LEVELS + 2) * c] for ci in order}
    upd = {ci: lax.dot_general(rows(v, ci), (rows(kk, ci) * e_out[ci]).astype(BF16), (((0,), (0,)), ((), ())),
                               preferred_element_type=F32) for ci in order}
    o_intra = {ci: jnp.dot(a[ci].astype(BF16), rows(v, ci), preferred_element_type=F32) for ci in order}
    qd = {ci: (rows(qf, ci) * e_in[ci]).astype(BF16) for ci in order}
    out = {}
    for ci in order:
        out[ci] = o_intra[ci] + lax.dot_general(qd[ci], st.astype(BF16), dn_nt, preferred_element_type=F32)
        tot = e_in[ci][0:1] if reverse else e_in[ci][c - 1:c]
        st = st * tot + upd[ci]
    return out, st


def _hg_kernel(reverse, fuse_readout, nchunk, *refs):
    if fuse_readout:
        (q_ref, v_ref, z_ref, lb_ref, s0_ref, dmat_ref, qsel_ref, amask_ref, of_ref, og_ref, gn_ref,
         o_ref, sout_ref, st_scr) = refs
    else:
        (q_ref, v_ref, z_ref, lb_ref, s0_ref, dmat_ref, qsel_ref, amask_ref,
         o_ref, sout_ref, st_scr) = refs
    t = pl.program_id(2)

    @pl.when(t == 0)
    def _():
        st_scr[...] = s0_ref[0, 0]

    order = list(range(nchunk - 1, -1, -1) if reverse else range(nchunk))
    out, st = _hg_block(q_ref[0], z_ref[0], v_ref[0], lb_ref[0], st_scr[...], dmat_ref[...],
                        qsel_ref, amask_ref, order, reverse)
    o = jnp.concatenate([out[ci] for ci in range(nchunk)], axis=0)
    if fuse_readout:
        o = o + of_ref[0]
        on = o * lax.rsqrt(jnp.mean(o * o, axis=-1, keepdims=True) + EPS) * gn_ref[...]
        g = og_ref[0].astype(F32)
        o_ref[0] = (on * (g * jax.nn.sigmoid(g))).astype(o_ref.dtype)
    else:
        o_ref[0] = o
    st_scr[...] = st

    @pl.when(t == pl.num_programs(2) - 1)
    def _():
        sout_ref[0, 0] = st


def _hgrn(mix, fg, lb, s0, consts, reverse, d, o_fwd=None, gn=None):
    b, s, _ = mix.shape
    tt = _pick(s, (1024, 512, 256, 128, 64))
    nt = s // tt
    base = 2 * d // HG_DIM
    fuse = o_fwd is not None
    dmat, qsel, amask = consts
    fcol = HG_HEADS if reverse else 0

    def tmap(t):
        return nt - 1 - t if reverse else t

    blk = (1, tt, HG_DIM)
    in_specs = [pl.BlockSpec(blk, lambda bi, h, t: (bi, tmap(t), base + h)),
                pl.BlockSpec(blk, lambda bi, h, t: (bi, tmap(t), base + HG_HEADS + h)),
                pl.BlockSpec(blk, lambda bi, h, t: (bi, tmap(t), fcol + h)),
                pl.BlockSpec((1, 1, HG_DIM), lambda bi, h, t: (h, 0, 0)),
                pl.BlockSpec((1, 1, HG_DIM, HG_DIM), lambda bi, h, t: (bi, h, 0, 0)),
                pl.BlockSpec(dmat.shape, lambda bi, h, t: (0, 0)),
                pl.BlockSpec(qsel.shape, lambda bi, h, t: (0, 0, 0)),
                pl.BlockSpec(amask.shape, lambda bi, h, t: (0, 0, 0))]
    args = [mix, mix, fg, lb.reshape(HG_HEADS, 1, HG_DIM), s0, dmat, qsel, amask]
    if fuse:
        in_specs += [pl.BlockSpec(blk, lambda bi, h, t: (bi, tmap(t), h)),
                     pl.BlockSpec(blk, lambda bi, h, t: (bi, tmap(t), base + 2 * HG_HEADS + h)),
                     pl.BlockSpec((1, HG_DIM), lambda bi, h, t: (0, 0))]
        args += [o_fwd, mix, gn.reshape(1, HG_DIM)]
    return pl.pallas_call(
        functools.partial(_hg_kernel, reverse, fuse, tt // HG_CHUNK),
        grid=(b, HG_HEADS, nt),
        in_specs=in_specs,
        out_specs=[pl.BlockSpec(blk, lambda bi, h, t: (bi, tmap(t), h)),
                   pl.BlockSpec((1, 1, HG_DIM, HG_DIM), lambda bi, h, t: (bi, h, 0, 0))],
        out_shape=[jax.ShapeDtypeStruct((b, s, HG_WIDTH), BF16 if fuse else F32),
                   jax.ShapeDtypeStruct((b, HG_HEADS, HG_DIM, HG_DIM), F32)],
        scratch_shapes=[pltpu.VMEM((HG_DIM, HG_DIM), F32)],
        compiler_params=_cparams(("parallel", "parallel", "arbitrary")),
        name="hgrn_bwd" if reverse else "hgrn_fwd",
    )(*args)


def _merge_kernel(oa_ref, ob_ref, ga_ref, gb_ref, x_ref, g1_ref, n2_ref, sh2_ref, sc2_ref,
                  wpa_ref, wpb_ref, wo_ref, x1_ref, h2_ref):
    ya = jnp.dot(oa_ref[0], wpa_ref[...], preferred_element_type=F32)
    yb = jnp.dot(ob_ref[0], wpb_ref[...], preferred_element_type=F32)
    y = (jax.nn.sigmoid(ga_ref[0].astype(F32)) * ya + jax.nn.sigmoid(gb_ref[0].astype(F32)) * yb)
    z = jnp.dot(y.astype(BF16), wo_ref[...], preferred_element_type=F32)
    x1 = x_ref[0] + g1_ref[0] * z
    x1_ref[0] = x1
    ms = jnp.mean(x1 * x1, axis=-1, keepdims=True)
    hn = x1 * lax.rsqrt(ms + EPS) * n2_ref[...]
    h2_ref[0] = (hn * (1.0 + sc2_ref[0]) + sh2_ref[0]).astype(BF16)


def _merge(o_a, o_b, mix, x, g1, n2, sh2, sc2, wpa, wpb, wo):
    b, s, d = x.shape
    tm = _pick(s, (256, 128))
    const = lambda shape: pl.BlockSpec(shape, lambda bi, i: (0,) * len(shape), pipeline_mode=pl.Buffered(1))
    vec = pl.BlockSpec((1, 1, d), lambda bi, i: (bi, 0, 0))
    return pl.pallas_call(
        _merge_kernel,
        grid=(b, s // tm),
        in_specs=[pl.BlockSpec((1, tm, NA_WIDTH), lambda bi, i: (bi, i, 0)),
                  pl.BlockSpec((1, tm, HG_WIDTH), lambda bi, i: (bi, i, 0)),
                  pl.BlockSpec((1, tm, d), lambda bi, i: (bi, i, 0)),
                  pl.BlockSpec((1, tm, d), lambda bi, i: (bi, i, 1)),
                  pl.BlockSpec((1, tm, d), lambda bi, i: (bi, i, 0)),
                  vec, const((1, d)), vec, vec,
                  const(wpa.shape), const(wpb.shape), const(wo.shape)],
        out_specs=[pl.BlockSpec((1, tm, d), lambda bi, i: (bi, i, 0)),
                   pl.BlockSpec((1, tm, d), lambda bi, i: (bi, i, 0))],
        out_shape=[jax.ShapeDtypeStruct((b, s, d), F32), jax.ShapeDtypeStruct((b, s, d), BF16)],
        compiler_params=_cparams(("parallel", "parallel")),
        name="merge",
    )(o_a, o_b, mix, mix, x, g1, n2, sh2, sc2, wpa, wpb, wo)


def _ffn_kernel(h_ref, x1_ref, g2_ref, fg_ref, wa_ref, wu_ref, wo_ref, o_ref):
    j = pl.program_id(2)
    h = h_ref[0]
    a = jnp.dot(h, wa_ref[...], preferred_element_type=F32)
    u = jnp.dot(h, wu_ref[...], preferred_element_type=F32)
    gate = (a * jax.nn.sigmoid(a) * u).astype(BF16)

    @pl.when(j == 0)
    def _():
        o_ref[0] = jnp.zeros(o_ref.shape[1:], F32)

    d = o_ref.shape[2]
    nc = _pick(d, (512, 256, 128))
    for n0 in range(0, d, nc):
        o_ref[0, :, n0:n0 + nc] += jnp.dot(gate, wo_ref[:, n0:n0 + nc], preferred_element_type=F32)

    @pl.when(j == pl.num_programs(2) - 1)
    def _():
        x2 = x1_ref[0] + g2_ref[0] * o_ref[0]
        ms = jnp.mean(x2 * x2, axis=-1, keepdims=True)
        o_ref[0] = x2 * lax.rsqrt(ms + EPS) * fg_ref[...]


def _ffn(h2, x1, g2, final_g, w_in, w_out):
    b, s, d = x1.shape
    fh = w_out.shape[0]
    tf = _pick(fh, (512, 256, 128))
    nf = fh // tf
    tm = _pick(s, (1024, 512, 256, 128))
    return pl.pallas_call(
        _ffn_kernel,
        grid=(b, s // tm, nf),
        in_specs=[pl.BlockSpec((1, tm, d), lambda bi, i, j: (bi, i, 0)),
                  pl.BlockSpec((1, tm, d), lambda bi, i, j: (bi, i, 0), pipeline_mode=pl.Buffered(1)),
                  pl.BlockSpec((1, 1, d), lambda bi, i, j: (bi, 0, 0)),
                  pl.BlockSpec((1, d), lambda bi, i, j: (0, 0)),
                  pl.BlockSpec((d, tf), lambda bi, i, j: (0, j)),
                  pl.BlockSpec((d, tf), lambda bi, i, j: (0, nf + j)),
                  pl.BlockSpec((tf, d), lambda bi, i, j: (j, 0))],
        out_specs=pl.BlockSpec((1, tm, d), lambda bi, i, j: (bi, i, 0)),
        out_shape=jax.ShapeDtypeStruct((b, s, d), F32),
        compiler_params=_cparams(("parallel", "parallel", "arbitrary")),
        name="ffn",
    )(h2, x1, g2, final_g.reshape(1, d), w_in, w_in, w_out)


def kernel(x, c, ctx, c_ctx, w_ada, b_ada, norm1_g, w_in, na_rpb, hg_lb_logits, hg_norm_g,
           w_pa, w_pb, w_out, norm2_g, w_ffn_in, w_ffn_out, final_g):
    b, s, d = x.shape
    assert w_ada.shape[0] == 1, "single layer"

    lb_table = jnp.cumsum(jax.nn.softmax(hg_lb_logits.astype(F32), axis=0), axis=0)
    lb_f, lb_b = lb_table[0, 0], lb_table[0, 1]

    c8 = jnp.zeros((8, d), F32).at[:b].set(c).at[b].set(c_ctx)
    mod = _ada(c8, w_ada[0], b_ada[0]).reshape(8, N_MOD, d)
    sh1, sc1, g1, sh2, sc2, g2 = [mod[:b, j][:, None, :] for j in range(N_MOD)]
    sh1c = jnp.broadcast_to(mod[b, 0][None, None, :], (b, 1, d))
    sc1c = jnp.broadcast_to(mod[b, 1][None, None, :], (b, 1, d))

    o = np.cumsum((0, NA_WIDTH, NA_WIDTH, NA_WIDTH, HG_WIDTH, HG_WIDTH, HG_WIDTH, HG_WIDTH, HG_WIDTH, d, d))
    seg = lambda i: w_in[0][:, o[i]:o[i + 1]]
    w_perm = jnp.concatenate([seg(0), seg(1), seg(2), seg(8), seg(9), seg(3), seg(6), seg(7), seg(4), seg(5)],
                             axis=1).astype(BF16)
    n1 = norm1_g[0].reshape(1, d)

    qkv, mix, fg = _inproj(_prenorm(x, n1, sh1, sc1), w_perm)
    qkv_c, mix_c, fg_c = _inproj(_prenorm(ctx, n1, sh1c, sc1c), w_perm)

    o_na = _na(qkv, qkv_c, _na_bias_table(na_rpb[0]))

    cf, cb = _hg_constants(False), _hg_constants(True)
    zero = jnp.zeros((b, HG_HEADS, HG_DIM, HG_DIM), F32)
    _, s_f = _hgrn(mix_c, fg_c, lb_f, zero, cf, False, d)
    _, s_b = _hgrn(mix_c, fg_c, lb_b, zero, cb, True, d)
    o_f, _ = _hgrn(mix, fg, lb_f, s_f, cf, False, d)
    o_hg, _ = _hgrn(mix, fg, lb_b, s_b, cb, True, d, o_fwd=o_f, gn=hg_norm_g[0])

    x1, h2 = _merge(o_na, o_hg, mix, x, g1, norm2_g[0].reshape(1, d), sh2, sc2,
                    w_pa[0].astype(BF16), w_pb[0].astype(BF16), w_out[0].astype(BF16))
    return _ffn(h2, x1, g2, final_g, w_ffn_in[0].astype(BF16), w_ffn_out[0].astype(BF16))
```

```python
import functools

import jax
import jax.numpy as jnp
import numpy as np
from jax import lax
from jax.experimental import pallas as pl
from jax.experimental.pallas import tpu as pltpu

F32 = jnp.float32
BF16 = jnp.bfloat16

GRID_W = 64
WIN_H = 8
WIN_W = 16
NA_HEADS = 16
NA_HEAD_DIM = 64
NA_WIDTH = NA_HEADS * NA_HEAD_DIM
HG_HEADS = 8
HG_DIM = 128
HG_WIDTH = HG_HEADS * HG_DIM
N_MOD = 6
EPS = 1e-6
NEG_BIG = -1e30

V7X_VMEM_LIMIT_BYTES = 60000 * 1024
MXU_ROWS_PER_DOT = 512
NA_PAIR_GROUP = 4
HG_CHUNK = 64
HG_LEVELS = 6


def _cparams(sem):
    return pltpu.CompilerParams(dimension_semantics=sem, vmem_limit_bytes=V7X_VMEM_LIMIT_BYTES)


def _pick(n, prefs):
    for p in prefs:
        if n % p == 0:
            return p
    return n


def _ada_kernel(c_ref, w_ref, b_ref, o_ref):
    c = c_ref[...]
    s = c * jax.nn.sigmoid(c)
    o_ref[...] = jnp.dot(s, w_ref[...], precision=lax.Precision.HIGHEST,
                         preferred_element_type=F32) + b_ref[...]


def _ada(c8, w, b):
    d, n = w.shape
    tn = _pick(n, (1024, 512, 256, 128))
    return pl.pallas_call(
        _ada_kernel,
        grid=(n // tn,),
        in_specs=[pl.BlockSpec((8, d), lambda j: (0, 0)),
                  pl.BlockSpec((d, tn), lambda j: (0, j)),
                  pl.BlockSpec((1, tn), lambda j: (0, j))],
        out_specs=pl.BlockSpec((8, tn), lambda j: (0, j)),
        out_shape=jax.ShapeDtypeStruct((8, n), F32),
        compiler_params=_cparams(("arbitrary",)),
        name="ada",
    )(c8, w, b.reshape(1, n))


def _prenorm_kernel(x_ref, g_ref, sh_ref, sc_ref, h_ref):
    x = x_ref[0]
    ms = jnp.mean(x * x, axis=-1, keepdims=True)
    y = x * lax.rsqrt(ms + EPS) * g_ref[...]
    h_ref[0] = (y * (1.0 + sc_ref[0]) + sh_ref[0]).astype(BF16)


def _prenorm(x, g, sh, sc):
    b, t, d = x.shape
    tm = _pick(t, (512, 256))
    vec = pl.BlockSpec((1, 1, d), lambda bi, i: (bi, 0, 0))
    return pl.pallas_call(
        _prenorm_kernel,
        grid=(b, t // tm),
        in_specs=[pl.BlockSpec((1, tm, d), lambda bi, i: (bi, i, 0)),
                  pl.BlockSpec((1, d), lambda bi, i: (0, 0)), vec, vec],
        out_specs=pl.BlockSpec((1, tm, d), lambda bi, i: (bi, i, 0)),
        out_shape=jax.ShapeDtypeStruct((b, t, d), BF16),
        compiler_params=_cparams(("parallel", "parallel")),
        name="prenorm",
    )(x, g, sh, sc)


def _inproj_kernel(n_qkv, n_mix, h_ref, w_ref, qkv_ref, mix_ref, fg_ref):
    j = pl.program_id(2)
    tm = h_ref.shape[1]
    rows = min(tm, MXU_ROWS_PER_DOT)

    def project(o_ref):
        for r0 in range(0, tm, rows):
            o_ref[0, r0:r0 + rows] = jnp.dot(h_ref[0, r0:r0 + rows], w_ref[...],
                                             preferred_element_type=F32).astype(o_ref.dtype)

    @pl.when(j < n_qkv)
    def _():
        project(qkv_ref)

    @pl.when((j >= n_qkv) & (j < n_qkv + n_mix))
    def _():
        project(mix_ref)

    @pl.when(j >= n_qkv + n_mix)
    def _():
        project(fg_ref)


def _inproj(h, w):
    b, t, d = h.shape
    w_qkv, w_mix, w_fg = 3 * NA_WIDTH, 2 * d + 3 * HG_WIDTH, 2 * HG_WIDTH
    tn = 1024
    assert w_qkv % tn == 0 and w_mix % tn == 0 and w_fg % tn == 0
    n_qkv, n_mix, n_fg = w_qkv // tn, w_mix // tn, w_fg // tn
    tm = _pick(t, (1024, 512, 256))
    return pl.pallas_call(
        functools.partial(_inproj_kernel, n_qkv, n_mix),
        grid=(b, t // tm, n_qkv + n_mix + n_fg),
        in_specs=[pl.BlockSpec((1, tm, d), lambda bi, i, j: (bi, i, 0)),
                  pl.BlockSpec((d, tn), lambda bi, i, j: (0, j))],
        out_specs=[pl.BlockSpec((1, tm, tn), lambda bi, i, j: (bi, i, jnp.minimum(j, n_qkv - 1))),
                   pl.BlockSpec((1, tm, tn), lambda bi, i, j: (bi, i, jnp.clip(j - n_qkv, 0, n_mix - 1))),
                   pl.BlockSpec((1, tm, tn), lambda bi, i, j: (bi, i, jnp.maximum(j - n_qkv - n_mix, 0)))],
        out_shape=[jax.ShapeDtypeStruct((b, t, w_qkv), BF16),
                   jax.ShapeDtypeStruct((b, t, w_mix), BF16),
                   jax.ShapeDtypeStruct((b, t, w_fg), F32)],
        compiler_params=_cparams(("parallel", "parallel", "arbitrary")),
        name="inproj",
    )(h, w)


def _na_kernel(nb, q_ref, kb_ref, vb_ref, kc_ref, vc_ref, bias_ref, o_ref):
    dn = (((1,), (1,)), ((), ()))
    first = lax.broadcasted_iota(jnp.int32, (GRID_W, 2 * NA_HEAD_DIM), 1) < NA_HEAD_DIM
    zero = jnp.zeros((GRID_W, 2 * NA_HEAD_DIM), BF16)
    for g0 in range(0, NA_HEADS // 2, NA_PAIR_GROUP):
        pairs = range(g0, g0 + NA_PAIR_GROUP)
        heads = [(p, a) for p in pairs for a in (0, 1)]
        cols = {p: slice(2 * p * NA_HEAD_DIM, 2 * (p + 1) * NA_HEAD_DIM) for p in pairs}
        q2 = {p: q_ref[0, :, cols[p]] * (NA_HEAD_DIM ** -0.5) for p in pairs}
        q = {(p, a): jnp.where(first, q2[p], zero) if a == 0 else jnp.where(first, zero, q2[p]) for p, a in heads}
        s_b = {(p, a): lax.dot_general(q[p, a], kb_ref[0, :, cols[p]], dn, preferred_element_type=F32)
               + bias_ref[0, 2 * p + a] for p, a in heads}
        s_c = {(p, a): lax.dot_general(q[p, a], kc_ref[0, :, cols[p]], dn, preferred_element_type=F32)
               for p, a in heads}
        m = {h: jnp.maximum(jnp.max(s_b[h], axis=-1, keepdims=True), jnp.max(s_c[h], axis=-1, keepdims=True))
             for h in heads}
        p_b = {h: jnp.exp(s_b[h] - m[h]) for h in heads}
        p_c = {h: jnp.exp(s_c[h] - m[h]) for h in heads}
        l = {h: jnp.sum(p_b[h], axis=-1, keepdims=True) + jnp.sum(p_c[h], axis=-1, keepdims=True) for h in heads}
        o = {(p, a): jnp.dot(p_b[p, a].astype(BF16), vb_ref[0, :, cols[p]], preferred_element_type=F32)
             + jnp.dot(p_c[p, a].astype(BF16), vc_ref[0, :, cols[p]], preferred_element_type=F32) for p, a in heads}
        for p in pairs:
            o_ref[0, :, cols[p]] = jnp.where(first, o[p, 0] / l[p, 0], o[p, 1] / l[p, 1]).astype(BF16)


def _na_bias_table(rpb):
    col = np.arange(GRID_W)
    col_start = np.clip(col - WIN_W // 2, 0, GRID_W - WIN_W)
    in_win = (col[None, :] >= col_start[:, None]) & (col[None, :] < col_start[:, None] + WIN_W)
    dc_idx = np.clip(col[None, :] - col[:, None], 1 - WIN_W, WIN_W - 1) + WIN_W - 1
    g = rpb.astype(F32)[:, :, dc_idx]
    g = jnp.where(jnp.asarray(in_win)[None, None], g, NEG_BIG)
    tabs = [g[:, off:off + WIN_H].transpose(0, 2, 1, 3).reshape(NA_HEADS, GRID_W, WIN_H * GRID_W)
            for off in range(WIN_H)]
    return jnp.stack(tabs)


def _na(qkv, qkv_c, bias_tab):
    b, s, _ = qkv.shape
    lc = qkv_c.shape[1]
    rows = s // GRID_W
    assert rows >= WIN_H
    nb = WIN_H * GRID_W

    def rs_of(r):
        return jnp.clip(r - WIN_H // 2, 0, rows - WIN_H)

    band = (pl.Element(1), pl.Element(nb), pl.Element(NA_WIDTH))

    return pl.pallas_call(
        functools.partial(_na_kernel, nb),
        grid=(b, rows),
        in_specs=[pl.BlockSpec((1, GRID_W, NA_WIDTH), lambda bi, r: (bi, r, 0)),
                  pl.BlockSpec(band, lambda bi, r: (bi, rs_of(r) * GRID_W, NA_WIDTH)),
                  pl.BlockSpec(band, lambda bi, r: (bi, rs_of(r) * GRID_W, 2 * NA_WIDTH)),
                  pl.BlockSpec((1, lc, NA_WIDTH), lambda bi, r: (bi, 0, 1)),
                  pl.BlockSpec((1, lc, NA_WIDTH), lambda bi, r: (bi, 0, 2)),
                  pl.BlockSpec((1, NA_HEADS, GRID_W, nb),
                               lambda bi, r: (rs_of(r) - r + WIN_H - 1, 0, 0, 0))],
        out_specs=pl.BlockSpec((1, GRID_W, NA_WIDTH), lambda bi, r: (bi, r, 0)),
        out_shape=jax.ShapeDtypeStruct((b, s, NA_WIDTH), BF16),
        compiler_params=_cparams(("parallel", "arbitrary")),
        name="na",
    )(qkv, qkv, qkv, qkv_c, qkv_c, bias_tab)


def _hg_constants(reverse):
    c = HG_CHUNK
    pos = np.arange(c)
    dmat = np.zeros((HG_LEVELS + 2, c, c), np.float32)
    qsel = np.zeros((HG_LEVELS, c, HG_DIM), np.float32)
    amask = np.zeros((HG_LEVELS + 1, c, c), np.float32)
    for l in range(HG_LEVELS):
        m = 1 << l
        blk = pos // (2 * m)
        bnd = blk * 2 * m + m - 1
        is_q = (pos % (2 * m)) >= m
        qsel[l] = is_q[:, None]
        for t in range(c):
            if is_q[t]:
                dmat[l, t, bnd[t] + 1:t + 1] = 1.0
            else:
                dmat[l, t, t + 1:bnd[t] + 1] = 1.0
        amask[l] = (blk[:, None] == blk[None, :]) & is_q[:, None] & ~is_q[None, :]
    for t in range(c):
        dmat[HG_LEVELS, t, :t + 1] = 1.0
        dmat[HG_LEVELS + 1, t, t + 1:] = 1.0
    amask[HG_LEVELS] = np.eye(c)
    if reverse:
        dmat = dmat[:, ::-1, ::-1]
        qsel = qsel[:, ::-1]
        amask = amask[:, ::-1, ::-1]
    dmat = dmat.reshape((HG_LEVELS + 2) * c, c)
    return (jnp.asarray(np.concatenate([dmat, dmat, dmat], axis=1), BF16),
            jnp.asarray(qsel, F32), jnp.asarray(amask, F32))


def _split3(a):
    hi = a.astype(BF16)
    r1 = a - hi.astype(F32)
    mid = r1.astype(BF16)
    lo = (r1 - mid.astype(F32)).astype(BF16)
    return hi, mid, lo


def _hg_block(q, z, v, lb, st, dmat, qsel_ref, amask_ref, order, reverse):
    c = HG_CHUNK
    dn_nt = (((1,), (1,)), ((), ()))
    f = lb + (1.0 - lb) * jax.nn.sigmoid(z)
    lf = jnp.log(f)
    kk = 1.0 - f
    qf = q.astype(F32)
    hi, mid, lo = _split3(lf)
    rows = lambda a, ci: a[ci * c:(ci + 1) * c]
    expo = {}
    for ca, cb in zip(order[0::2], order[1::2]):
        pieces = jnp.concatenate(
            [jnp.concatenate([rows(a, ca), rows(a, cb)], axis=1) for a in (hi, mid, lo)], axis=0)
        both = jnp.dot(dmat, pieces, preferred_element_type=F32)
        expo[ca], expo[cb] = both[:, :HG_DIM], both[:, HG_DIM:]
    if len(order) % 2:
        ci = order[-1]
        expo[ci] = jnp.dot(dmat, jnp.concatenate([rows(a, ci) for a in (hi, mid, lo)], axis=0),
                           preferred_element_type=F32)
    e = {ci: jnp.exp(expo[ci]) for ci in order}
    qsel = [qsel_ref[l] != 0.0 for l in range(HG_LEVELS)]
    amask = [amask_ref[l] != 0.0 for l in range(HG_LEVELS + 1)]
    a = {ci: lax.dot_general(rows(q, ci), rows(kk, ci).astype(BF16), dn_nt, preferred_element_type=F32)
         for ci in order}
    a = {ci: jnp.where(amask[HG_LEVELS], a[ci], 0.0) for ci in order}
    for l in range(HG_LEVELS):
        w = {ci: (e[ci][l * c:(l + 1) * c] * jnp.where(qsel[l], rows(qf, ci), rows(kk, ci))).astype(BF16)
             for ci in order}
        p = {ci: lax.dot_general(w[ci], w[ci], dn_nt, preferred_element_type=F32) for ci in order}
        a = {ci: jnp.where(amask[l], p[ci], a[ci]) for ci in order}
    e_in = {ci: e[ci][HG_LEVELS * c:(HG_LEVELS + 1) * c] for ci in order}
    e_out = {ci: e[ci][(HG_LEVELS + 1) * c:(HG_LEVELS + 2) * c] for ci in order}
    upd = {ci: lax.dot_general(rows(v, ci), (rows(kk, ci) * e_out[ci]).astype(BF16), (((0,), (0,)), ((), ())),
                               preferred_element_type=F32) for ci in order}
    o_intra = {ci: jnp.dot(a[ci].astype(BF16), rows(v, ci), preferred_element_type=F32) for ci in order}
    qd = {ci: (rows(qf, ci) * e_in[ci]).astype(BF16) for ci in order}
    out = {}
    for ci in order:
        out[ci] = o_intra[ci] + lax.dot_general(qd[ci], st.astype(BF16), dn_nt, preferred_element_type=F32)
        tot = e_in[ci][0:1] if reverse else e_in[ci][c - 1:c]
        st = st * tot + upd[ci]
    return out, st


def _hg_kernel(reverse, fuse_readout, nchunk, *refs):
    if fuse_readout:
        (q_ref, v_ref, z_ref, lb_ref, s0_ref, dmat_ref, qsel_ref, amask_ref, of_ref, og_ref, gn_ref,
         o_ref, sout_ref, st_scr) = refs
    else:
        (q_ref, v_ref, z_ref, lb_ref, s0_ref, dmat_ref, qsel_ref, amask_ref,
         o_ref, sout_ref, st_scr) = refs
    t = pl.program_id(2)

    @pl.when(t == 0)
    def _():
        st_scr[...] = s0_ref[0, 0]

    order = list(range(nchunk - 1, -1, -1) if reverse else range(nchunk))
    out, st = _hg_block(q_ref[0], z_ref[0], v_ref[0], lb_ref[0], st_scr[...], dmat_ref[...],
                        qsel_ref, amask_ref, order, reverse)
    o = jnp.concatenate([out[ci] for ci in range(nchunk)], axis=0)
    if fuse_readout:
        o = o + of_ref[0]
        on = o * lax.rsqrt(jnp.mean(o * o, axis=-1, keepdims=True) + EPS) * gn_ref[...]
        g = og_ref[0].astype(F32)
        o_ref[0] = (on * (g * jax.nn.sigmoid(g))).astype(o_ref.dtype)
    else:
        o_ref[0] = o
    st_scr[...] = st

    @pl.when(t == pl.num_programs(2) - 1)
    def _():
        sout_ref[0, 0] = st


def _hgrn(mix, fg, lb, s0, consts, reverse, d, o_fwd=None, gn=None):
    b, s, _ = mix.shape
    tt = _pick(s, (1024, 512, 256, 128, 64))
    nt = s // tt
    base = 2 * d // HG_DIM
    fuse = o_fwd is not None
    dmat, qsel, amask = consts
    fcol = HG_HEADS if reverse else 0

    def tmap(t):
        return nt - 1 - t if reverse else t

    blk = (1, tt, HG_DIM)
    in_specs = [pl.BlockSpec(blk, lambda bi, h, t: (bi, tmap(t), base + h)),
                pl.BlockSpec(blk, lambda bi, h, t: (bi, tmap(t), base + HG_HEADS + h)),
                pl.BlockSpec(blk, lambda bi, h, t: (bi, tmap(t), fcol + h)),
                pl.BlockSpec((1, 1, HG_DIM), lambda bi, h, t: (h, 0, 0)),
                pl.BlockSpec((1, 1, HG_DIM, HG_DIM), lambda bi, h, t: (bi, h, 0, 0)),
                pl.BlockSpec(dmat.shape, lambda bi, h, t: (0, 0)),
                pl.BlockSpec(qsel.shape, lambda bi, h, t: (0, 0, 0)),
                pl.BlockSpec(amask.shape, lambda bi, h, t: (0, 0, 0))]
    args = [mix, mix, fg, lb.reshape(HG_HEADS, 1, HG_DIM), s0, dmat, qsel, amask]
    if fuse:
        in_specs += [pl.BlockSpec(blk, lambda bi, h, t: (bi, tmap(t), h)),
                     pl.BlockSpec(blk, lambda bi, h, t: (bi, tmap(t), base + 2 * HG_HEADS + h)),
                     pl.BlockSpec((1, HG_DIM), lambda bi, h, t: (0, 0))]
        args += [o_fwd, mix, gn.reshape(1, HG_DIM)]
    return pl.pallas_call(
        functools.partial(_hg_kernel, reverse, fuse, tt // HG_CHUNK),
        grid=(b, HG_HEADS, nt),
        in_specs=in_specs,
        out_specs=[pl.BlockSpec(blk, lambda bi, h, t: (bi, tmap(t), h)),
                   pl.BlockSpec((1, 1, HG_DIM, HG_DIM), lambda bi, h, t: (bi, h, 0, 0))],
        out_shape=[jax.ShapeDtypeStruct((b, s, HG_WIDTH), BF16 if fuse else F32),
                   jax.ShapeDtypeStruct((b, HG_HEADS, HG_DIM, HG_DIM), F32)],
        scratch_shapes=[pltpu.VMEM((HG_DIM, HG_DIM), F32)],
        compiler_params=_cparams(("parallel", "parallel", "arbitrary")),
        name="hgrn_bwd" if reverse else "hgrn_fwd",
    )(*args)


def _merge_kernel(oa_ref, ob_ref, ga_ref, gb_ref, x_ref, g1_ref, n2_ref, sh2_ref, sc2_ref,
                  wpa_ref, wpb_ref, wo_ref, x1_ref, h2_ref, y_scr):
    tm, d = x1_ref.shape[1:]
    nc = _pick(d, (512, 256, 128))
    oa, ob = oa_ref[0], ob_ref[0]
    for n0 in range(0, d, nc):
        cs = slice(n0, n0 + nc)
        ya = jnp.dot(oa, wpa_ref[:, cs], preferred_element_type=F32)
        yb = jnp.dot(ob, wpb_ref[:, cs], preferred_element_type=F32)
        y_scr[:, cs] = (jax.nn.sigmoid(ga_ref[0, :, cs].astype(F32)) * ya
                        + jax.nn.sigmoid(gb_ref[0, :, cs].astype(F32)) * yb).astype(BF16)
    y = y_scr[...]
    ss = jnp.zeros((tm, 1), F32)
    for n0 in range(0, d, nc):
        cs = slice(n0, n0 + nc)
        x1 = x_ref[0, :, cs] + g1_ref[0, :, cs] * jnp.dot(y, wo_ref[:, cs], preferred_element_type=F32)
        x1_ref[0, :, cs] = x1
        ss = ss + jnp.sum(x1 * x1, axis=-1, keepdims=True)
    r = lax.rsqrt(ss * (1.0 / d) + EPS)
    h2_ref[0] = (x1_ref[0] * r * n2_ref[...] * (1.0 + sc2_ref[0]) + sh2_ref[0]).astype(BF16)


def _merge(o_a, o_b, mix, x, g1, n2, sh2, sc2, wpa, wpb, wo):
    b, s, d = x.shape
    tm = _pick(s, (512, 256, 128))
    const = lambda shape: pl.BlockSpec(shape, lambda bi, i: (0,) * len(shape), pipeline_mode=pl.Buffered(1))
    vec = pl.BlockSpec((1, 1, d), lambda bi, i: (bi, 0, 0))
    return pl.pallas_call(
        _merge_kernel,
        grid=(b, s // tm),
        in_specs=[pl.BlockSpec((1, tm, NA_WIDTH), lambda bi, i: (bi, i, 0)),
                  pl.BlockSpec((1, tm, HG_WIDTH), lambda bi, i: (bi, i, 0)),
                  pl.BlockSpec((1, tm, d), lambda bi, i: (bi, i, 0)),
                  pl.BlockSpec((1, tm, d), lambda bi, i: (bi, i, 1)),
                  pl.BlockSpec((1, tm, d), lambda bi, i: (bi, i, 0)),
                  vec, const((1, d)), vec, vec,
                  const(wpa.shape), const(wpb.shape), const(wo.shape)],
        out_specs=[pl.BlockSpec((1, tm, d), lambda bi, i: (bi, i, 0)),
                   pl.BlockSpec((1, tm, d), lambda bi, i: (bi, i, 0))],
        out_shape=[jax.ShapeDtypeStruct((b, s, d), F32), jax.ShapeDtypeStruct((b, s, d), BF16)],
        scratch_shapes=[pltpu.VMEM((tm, d), BF16)],
        compiler_params=_cparams(("parallel", "parallel")),
        name="merge",
    )(o_a, o_b, mix, mix, x, g1, n2, sh2, sc2, wpa, wpb, wo)


def _ffn_kernel(h_ref, x1_ref, g2_ref, fg_ref, wa_ref, wu_ref, wo_ref, o_ref):
    j = pl.program_id(2)
    tm, d = o_ref.shape[1:]
    rows = min(tm, MXU_ROWS_PER_DOT)
    nc = _pick(d, (512, 256, 128))

    def hidden_slice(first):
        for r0 in range(0, tm, rows):
            h = h_ref[0, r0:r0 + rows]
            a = jnp.dot(h, wa_ref[...], preferred_element_type=F32)
            u = jnp.dot(h, wu_ref[...], preferred_element_type=F32)
            gate = (a * jax.nn.sigmoid(a) * u).astype(BF16)
            for n0 in range(0, d, nc):
                part = jnp.dot(gate, wo_ref[:, n0:n0 + nc], preferred_element_type=F32)
                if first:
                    o_ref[0, r0:r0 + rows, n0:n0 + nc] = part
                else:
                    o_ref[0, r0:r0 + rows, n0:n0 + nc] += part

    @pl.when(j == 0)
    def _():
        hidden_slice(True)

    @pl.when(j > 0)
    def _():
        hidden_slice(False)

    @pl.when(j == pl.num_programs(2) - 1)
    def _():
        x2 = x1_ref[0] + g2_ref[0] * o_ref[0]
        ms = jnp.mean(x2 * x2, axis=-1, keepdims=True)
        o_ref[0] = x2 * lax.rsqrt(ms + EPS) * fg_ref[...]


def _ffn(h2, x1, g2, final_g, w_in, w_out):
    b, s, d = x1.shape
    fh = w_out.shape[0]
    tf = _pick(fh, (512, 256, 128))
    nf = fh // tf
    tm = _pick(s, (1024, 512, 256, 128))
    return pl.pallas_call(
        _ffn_kernel,
        grid=(b, s // tm, nf),
        in_specs=[pl.BlockSpec((1, tm, d), lambda bi, i, j: (bi, i, 0)),
                  pl.BlockSpec((1, tm, d), lambda bi, i, j: (bi, i, 0), pipeline_mode=pl.Buffered(1)),
                  pl.BlockSpec((1, 1, d), lambda bi, i, j: (bi, 0, 0)),
                  pl.BlockSpec((1, d), lambda bi, i, j: (0, 0)),
                  pl.BlockSpec((d, tf), lambda bi, i, j: (0, j)),
                  pl.BlockSpec((d, tf), lambda bi, i, j: (0, nf + j)),
                  pl.BlockSpec((tf, d), lambda bi, i, j: (j, 0))],
        out_specs=pl.BlockSpec((1, tm, d), lambda bi, i, j: (bi, i, 0)),
        out_shape=jax.ShapeDtypeStruct((b, s, d), F32),
        compiler_params=_cparams(("parallel", "parallel", "arbitrary")),
        name="ffn",
    )(h2, x1, g2, final_g.reshape(1, d), w_in, w_in, w_out)


def kernel(x, c, ctx, c_ctx, w_ada, b_ada, norm1_g, w_in, na_rpb, hg_lb_logits, hg_norm_g,
           w_pa, w_pb, w_out, norm2_g, w_ffn_in, w_ffn_out, final_g):
    b, s, d = x.shape
    assert w_ada.shape[0] == 1, "single layer"

    lb_table = jnp.cumsum(jax.nn.softmax(hg_lb_logits.astype(F32), axis=0), axis=0)
    lb_f, lb_b = lb_table[0, 0], lb_table[0, 1]

    c8 = jnp.zeros((8, d), F32).at[:b].set(c).at[b].set(c_ctx)
    mod = _ada(c8, w_ada[0], b_ada[0]).reshape(8, N_MOD, d)
    sh1, sc1, g1, sh2, sc2, g2 = [mod[:b, j][:, None, :] for j in range(N_MOD)]
    sh1c = jnp.broadcast_to(mod[b, 0][None, None, :], (b, 1, d))
    sc1c = jnp.broadcast_to(mod[b, 1][None, None, :], (b, 1, d))

    o = np.cumsum((0, NA_WIDTH, NA_WIDTH, NA_WIDTH, HG_WIDTH, HG_WIDTH, HG_WIDTH, HG_WIDTH, HG_WIDTH, d, d))
    seg = lambda i: w_in[0][:, o[i]:o[i + 1]]
    w_perm = jnp.concatenate([seg(0), seg(1), seg(2), seg(8), seg(9), seg(3), seg(6), seg(7), seg(4), seg(5)],
                             axis=1).astype(BF16)
    n1 = norm1_g[0].reshape(1, d)

    qkv, mix, fg = _inproj(_prenorm(x, n1, sh1, sc1), w_perm)
    qkv_c, mix_c, fg_c = _inproj(_prenorm(ctx, n1, sh1c, sc1c), w_perm)

    o_na = _na(qkv, qkv_c, _na_bias_table(na_rpb[0]))

    cf, cb = _hg_constants(False), _hg_constants(True)
    zero = jnp.zeros((b, HG_HEADS, HG_DIM, HG_DIM), F32)
    _, s_f = _hgrn(mix_c, fg_c, lb_f, zero, cf, False, d)
    _, s_b = _hgrn(mix_c, fg_c, lb_b, zero, cb, True, d)
    o_f, _ = _hgrn(mix, fg, lb_f, s_f, cf, False, d)
    o_hg, _ = _hgrn(mix, fg, lb_b, s_b, cb, True, d, o_fwd=o_f, gn=hg_norm_g[0])

    x1, h2 = _merge(o_na, o_hg, mix, x, g1, norm2_g[0].reshape(1, d), sh2, sc2,
                    w_pa[0].astype(BF16), w_pb[0].astype(BF16), w_out[0].astype(BF16))
    return _ffn(h2, x1, g2, final_g, w_ffn_in[0].astype(BF16), w_ffn_out[0].astype(BF16))
```

```python
import functools

import jax
import jax.numpy as jnp
import numpy as np
from jax import lax
from jax.experimental import pallas as pl
from jax.experimental.pallas import tpu as pltpu

F32 = jnp.float32
BF16 = jnp.bfloat16

GRID_W = 64
WIN_H = 8
WIN_W = 16
NA_HEADS = 16
NA_HEAD_DIM = 64
NA_WIDTH = NA_HEADS * NA_HEAD_DIM
HG_HEADS = 8
HG_DIM = 128
HG_WIDTH = HG_HEADS * HG_DIM
N_MOD = 6
EPS = 1e-6
NEG_BIG = -1e30

V7X_VMEM_LIMIT_BYTES = 60000 * 1024
MXU_ROWS_PER_DOT = 512
NA_SLAB_HEADS = 4
HG_CHUNK = 64
HG_LEVELS = 6
HG_VPU_LEVELS = 1
LOG2_E = 1.4426950408889634


def _cparams(sem):
    return pltpu.CompilerParams(dimension_semantics=sem, vmem_limit_bytes=V7X_VMEM_LIMIT_BYTES)


def _pick(n, prefs):
    for p in prefs:
        if n % p == 0:
            return p
    return n


def _ada_kernel(c_ref, w_ref, b_ref, o_ref):
    c = c_ref[...]
    s = c * jax.nn.sigmoid(c)
    o_ref[...] = jnp.dot(s, w_ref[...], precision=lax.Precision.HIGHEST,
                         preferred_element_type=F32) + b_ref[...]


def _ada(c8, w, b):
    d, n = w.shape
    tn = _pick(n, (1024, 512, 256, 128))
    return pl.pallas_call(
        _ada_kernel,
        grid=(n // tn,),
        in_specs=[pl.BlockSpec((8, d), lambda j: (0, 0)),
                  pl.BlockSpec((d, tn), lambda j: (0, j)),
                  pl.BlockSpec((1, tn), lambda j: (0, j))],
        out_specs=pl.BlockSpec((8, tn), lambda j: (0, j)),
        out_shape=jax.ShapeDtypeStruct((8, n), F32),
        compiler_params=_cparams(("arbitrary",)),
        name="ada",
    )(c8, w, b.reshape(1, n))


INPROJ_TN = 1024
NORM_CHUNKS = (8, 4, 2, 1)


def _norm_inproj_kernel(starts, nx, x_ref, g_ref, sh_ref, sc_ref, w_ref, *refs):
    o_refs, h_bufs = refs[:-2], refs[-2:]
    i, j = pl.program_id(1), pl.program_id(2)
    tm = h_bufs[0].shape[0]
    xr = tm // nx
    rows = min(tm, MXU_ROWS_PER_DOT)

    def normalise_chunk(h_fill):
        x = x_ref[0]
        ms = jnp.mean(x * x, axis=-1, keepdims=True)
        y = x * lax.rsqrt(ms + EPS) * g_ref[...]
        r0 = pl.multiple_of(jnp.minimum(j, nx - 1) * xr, xr)
        h_fill[pl.ds(r0, xr), :] = (y * (1.0 + sc_ref[0]) + sh_ref[0]).astype(BF16)

    def project(o_ref, h_fill, h_use):
        normalise_chunk(h_fill)
        for r0 in range(0, tm, rows):
            o_ref[0, r0:r0 + rows] = jnp.dot(h_use[r0:r0 + rows], w_ref[...],
                                             preferred_element_type=F32).astype(o_ref.dtype)

    pl.when(i == 0)(functools.partial(normalise_chunk, h_bufs[0]))
    for parity in (0, 1):
        for k, o_ref in enumerate(o_refs):
            pl.when((i > 0) & (i % 2 == parity) & (j >= starts[k]) & (j < starts[k + 1]))(
                functools.partial(project, o_ref, h_bufs[parity], h_bufs[1 - parity]))


def _mod_spec(d, grid_rank, row, j):
    if grid_rank == 2:
        return pl.BlockSpec((1, 1, d), lambda bi, i: (row(bi) * N_MOD + j, 0, 0))
    return pl.BlockSpec((1, 1, d), lambda bi, i, k: (row(bi) * N_MOD + j, 0, 0))


def _norm_inproj(x, g, mod, row, w, outputs):
    b, t, d = x.shape
    tn = INPROJ_TN
    tm = _pick(t, (1024, 512, 256))
    ni = t // tm
    counts = [len(blocks) for _, blocks in outputs]
    starts = [int(v) for v in np.cumsum([0] + counts)]
    nj = starts[-1]
    table = [blk for _, blocks in outputs for blk in blocks]
    nx = next(n for n in NORM_CHUNKS if n <= nj and tm % (16 * n) == 0)
    xr = tm // nx

    def w_block(i, j):
        r = jnp.int32(table[-1])
        for idx in range(len(table) - 2, -1, -1):
            r = jnp.where(j == idx, table[idx], r)
        return jnp.where(i == 0, table[0], r)

    def x_block(i, j):
        return jnp.where(i >= ni, ni * nx - 1, i * nx + jnp.minimum(j, nx - 1))

    def o_spec(k):
        return pl.BlockSpec((1, tm, tn), lambda bi, i, j: (
            bi, jnp.maximum(i - 1, 0), jnp.where(i == 0, 0, jnp.clip(j - starts[k], 0, counts[k] - 1))))

    return pl.pallas_call(
        functools.partial(_norm_inproj_kernel, starts, nx),
        grid=(b, ni + 1, nj),
        in_specs=[pl.BlockSpec((1, xr, d), lambda bi, i, j: (bi, x_block(i, j), 0)),
                  pl.BlockSpec((1, d), lambda bi, i, j: (0, 0)),
                  _mod_spec(d, 3, row, 0), _mod_spec(d, 3, row, 1),
                  pl.BlockSpec((d, tn), lambda bi, i, j: (0, w_block(i, j)))],
        out_specs=[o_spec(k) for k in range(len(outputs))],
        out_shape=[jax.ShapeDtypeStruct((b, t, n * tn), dt) for (dt, _), n in zip(outputs, counts)],
        scratch_shapes=[pltpu.VMEM((tm, d), BF16), pltpu.VMEM((tm, d), BF16)],
        compiler_params=_cparams(("parallel", "arbitrary", "arbitrary")),
        name="norm_inproj",
    )(x, g, mod, mod, w)


def _na_kernel(nb, q_ref, kb_ref, vb_ref, kc_ref, vc_ref, bias_ref, o_ref):
    dn = (((1,), (1,)), ((), ()))
    nh, w = NA_SLAB_HEADS, NA_SLAB_HEADS * NA_HEAD_DIM
    lane_head = lax.broadcasted_iota(jnp.int32, (GRID_W, w), 1) // NA_HEAD_DIM
    own = [lane_head == a for a in range(nh)]
    slabs = range(NA_HEADS // nh)
    cols = {g: slice(g * w, (g + 1) * w) for g in slabs}
    q = {}
    for g in slabs:
        qg = q_ref[0, :, cols[g]] * (NA_HEAD_DIM ** -0.5)
        q[g] = jnp.concatenate([jnp.where(own[a], qg, jnp.zeros_like(qg)) for a in range(nh)], axis=0)
    s_b = {g: lax.dot_general(q[g], kb_ref[0, :, cols[g]], dn, preferred_element_type=F32)
           + bias_ref[0, g * nh:(g + 1) * nh].reshape(nh * GRID_W, nb) for g in slabs}
    s_c = {g: lax.dot_general(q[g], kc_ref[0, :, cols[g]], dn, preferred_element_type=F32) for g in slabs}
    m = {g: jnp.maximum(jnp.max(s_b[g], axis=-1, keepdims=True), jnp.max(s_c[g], axis=-1, keepdims=True))
         for g in slabs}
    p_b = {g: jnp.exp(s_b[g] - m[g]) for g in slabs}
    p_c = {g: jnp.exp(s_c[g] - m[g]) for g in slabs}
    l = {g: jnp.sum(p_b[g], axis=-1, keepdims=True) + jnp.sum(p_c[g], axis=-1, keepdims=True) for g in slabs}
    o = {g: (jnp.dot(p_b[g].astype(BF16), vb_ref[0, :, cols[g]], preferred_element_type=F32)
             + jnp.dot(p_c[g].astype(BF16), vc_ref[0, :, cols[g]], preferred_element_type=F32)) / l[g]
         for g in slabs}
    for g in slabs:
        out = o[g][0:GRID_W]
        for a in range(1, nh):
            out = jnp.where(own[a], o[g][a * GRID_W:(a + 1) * GRID_W], out)
        o_ref[0, :, cols[g]] = out.astype(BF16)


def _na_bias_table(rpb):
    col = np.arange(GRID_W)
    col_start = np.clip(col - WIN_W // 2, 0, GRID_W - WIN_W)
    in_win = (col[None, :] >= col_start[:, None]) & (col[None, :] < col_start[:, None] + WIN_W)
    dc_idx = np.clip(col[None, :] - col[:, None], 1 - WIN_W, WIN_W - 1) + WIN_W - 1
    onehot = (dc_idx[None] == np.arange(2 * WIN_W - 1)[:, None, None]).astype(np.float32)
    g = jnp.einsum('hdc,cqk->hdqk', rpb.astype(F32), jnp.asarray(onehot),
                   precision=lax.Precision.HIGHEST)
    g = jnp.where(jnp.asarray(in_win)[None, None], g, NEG_BIG)
    tabs = [g[:, off:off + WIN_H].transpose(0, 2, 1, 3).reshape(NA_HEADS, GRID_W, WIN_H * GRID_W)
            for off in range(WIN_H)]
    return jnp.stack(tabs)


def _na(qkv, kv_c, bias_tab):
    b, s, _ = qkv.shape
    lc = kv_c.shape[1]
    rows = s // GRID_W
    assert rows >= WIN_H
    nb = WIN_H * GRID_W

    def rs_of(r):
        return jnp.clip(r - WIN_H // 2, 0, rows - WIN_H)

    band = (pl.Element(1), pl.Element(nb), pl.Element(NA_WIDTH))

    return pl.pallas_call(
        functools.partial(_na_kernel, nb),
        grid=(b, rows),
        in_specs=[pl.BlockSpec((1, GRID_W, NA_WIDTH), lambda bi, r: (bi, r, 0)),
                  pl.BlockSpec(band, lambda bi, r: (bi, rs_of(r) * GRID_W, NA_WIDTH)),
                  pl.BlockSpec(band, lambda bi, r: (bi, rs_of(r) * GRID_W, 2 * NA_WIDTH)),
                  pl.BlockSpec((1, lc, NA_WIDTH), lambda bi, r: (bi, 0, 0)),
                  pl.BlockSpec((1, lc, NA_WIDTH), lambda bi, r: (bi, 0, 1)),
                  pl.BlockSpec((1, NA_HEADS, GRID_W, nb),
                               lambda bi, r: (rs_of(r) - r + WIN_H - 1, 0, 0, 0))],
        out_specs=pl.BlockSpec((1, GRID_W, NA_WIDTH), lambda bi, r: (bi, r, 0)),
        out_shape=jax.ShapeDtypeStruct((b, s, NA_WIDTH), BF16),
        compiler_params=_cparams(("parallel", "arbitrary")),
        name="na",
    )(qkv, qkv, qkv, kv_c, kv_c, bias_tab)


def _hg_constants(reverse):
    c = HG_CHUNK
    pos = np.arange(c)
    pmat = np.tril(np.ones((c, c), np.float32))
    qsel = np.zeros((HG_LEVELS, c, HG_DIM), np.float32)
    amask = np.zeros((HG_LEVELS - 1, c, c), np.float32)
    for l in range(HG_LEVELS):
        m = 1 << l
        blk = pos // (2 * m)
        is_q = (pos % (2 * m)) >= m
        qsel[l] = is_q[:, None]
        if l > 0:
            amask[l - 1] = (blk[:, None] == blk[None, :]) & is_q[:, None] & ~is_q[None, :]
    if reverse:
        pmat = pmat[::-1, ::-1]
        qsel = qsel[:, ::-1]
        amask = amask[:, ::-1, ::-1]
    return (jnp.asarray(np.concatenate([pmat, pmat, pmat], axis=1), BF16),
            jnp.asarray(qsel, F32), jnp.asarray(amask, F32))


def _boundary_rows(p, l, reverse):
    c = HG_CHUNK
    m = 1 << l
    off = m if reverse else m - 1
    if 2 * m >= 16:
        return jnp.concatenate([jnp.broadcast_to(p[b0 + off:b0 + off + 1], (2 * m, HG_DIM))
                                for b0 in range(0, c, 2 * m)], axis=0)
    p3 = p.reshape(c // 8, 8, HG_DIM)
    pick = lambda s: jnp.broadcast_to(p3[:, s:s + 1], p3.shape)
    if 2 * m == 8:
        return pick(off).reshape(c, HG_DIM)
    assert 2 * m == 4
    upper = lax.broadcasted_iota(jnp.int32, p3.shape, 1) >= 4
    return jnp.where(upper, pick(4 + off), pick(off)).reshape(c, HG_DIM)


def _chunk_prefixes(pmat, pieces, order):
    c = HG_CHUNK
    rows = lambda a, ci: a[ci * c:(ci + 1) * c]
    pre = {}
    for ca, cb in zip(order[0::2], order[1::2]):
        rhs = jnp.concatenate([jnp.concatenate([rows(a, ca), rows(a, cb)], axis=1) for a in pieces], axis=0)
        both = jnp.dot(pmat, rhs, preferred_element_type=F32)
        pre[ca], pre[cb] = both[:, :HG_DIM], both[:, HG_DIM:]
    if len(order) % 2:
        ci = order[-1]
        pre[ci] = jnp.dot(pmat, jnp.concatenate([rows(a, ci) for a in pieces], axis=0),
                          preferred_element_type=F32)
    return pre


def _split3(a):
    hi = a.astype(BF16)
    r1 = a - hi.astype(F32)
    mid = r1.astype(BF16)
    lo = (r1 - mid.astype(F32)).astype(BF16)
    return hi, mid, lo


def _hg_block(q, z, v, lb, st, pmat, qsel_ref, amask_ref, order, reverse):
    c = HG_CHUNK
    dn_nt = (((1,), (1,)), ((), ()))
    f = lb + (1.0 - lb) * jax.nn.sigmoid(z)
    kk = 1.0 - f
    qf = q.astype(F32)
    rows = lambda a, ci: a[ci * c:(ci + 1) * c]
    pre = _chunk_prefixes(pmat, _split3(jnp.log(f)), order)
    qsel = [qsel_ref[l] != 0.0 for l in range(HG_LEVELS)]
    amask = [amask_ref[l] != 0.0 for l in range(HG_LEVELS - 1)]
    a = {}
    for l in range(HG_VPU_LEVELS, HG_LEVELS):
        sign = jnp.where(qsel[l], LOG2_E, -LOG2_E)
        w = {ci: (jnp.exp2((pre[ci] - _boundary_rows(pre[ci], l, reverse)) * sign)
                  * jnp.where(qsel[l], rows(qf, ci), rows(kk, ci))).astype(BF16) for ci in order}
        p = {ci: lax.dot_general(w[ci], w[ci], dn_nt, preferred_element_type=F32) for ci in order}
        a = {ci: jnp.where(amask[l - 1], p[ci], a[ci] if l > HG_VPU_LEVELS else 0.0) for ci in order}
    last = 0 if reverse else c - 1
    e_in = {ci: jnp.exp(pre[ci]) for ci in order}
    e_out = {ci: jnp.exp(pre[ci][last:last + 1] - pre[ci]) for ci in order}
    upd = {ci: lax.dot_general(rows(v, ci), (rows(kk, ci) * e_out[ci]).astype(BF16), (((0,), (0,)), ((), ())),
                               preferred_element_type=F32) for ci in order}
    o_intra = {ci: jnp.dot(a[ci].astype(BF16), rows(v, ci), preferred_element_type=F32) for ci in order}
    vf = v.astype(F32)
    small = 1 << HG_VPU_LEVELS
    row = lax.broadcasted_iota(jnp.int32, (c, 1), 0)
    in_block = ((c - 1 - row) if reverse else row) % small
    back = lambda x, dist: pltpu.roll(x, (c - dist) if reverse else dist, 0)
    for ci in order:
        qc, kc, vc, fc = rows(qf, ci), rows(kk, ci), rows(vf, ci), rows(f, ci)
        acc = o_intra[ci] + jnp.sum(qc * kc, axis=-1, keepdims=True) * vc
        decay = fc
        for dist in range(1, small):
            if dist > 1:
                decay = decay * back(fc, dist - 1)
            wgt = jnp.sum(qc * decay * back(kc, dist), axis=-1, keepdims=True)
            acc = acc + jnp.where(in_block >= dist, wgt, 0.0) * back(vc, dist)
        o_intra[ci] = acc
    qd = {ci: (rows(qf, ci) * e_in[ci]).astype(BF16) for ci in order}
    out = {}
    for ci in order:
        out[ci] = o_intra[ci] + lax.dot_general(qd[ci], st.astype(BF16), dn_nt, preferred_element_type=F32)
        st = st * e_in[ci][last:last + 1] + upd[ci]
    return out, st


def _hg_kernel(reverse, fuse_readout, nchunk, *refs):
    if fuse_readout:
        (q_ref, v_ref, z_ref, lb_ref, s0_ref, dmat_ref, qsel_ref, amask_ref, of_ref, og_ref, gn_ref,
         o_ref, st_scr) = refs
    else:
        (q_ref, v_ref, z_ref, lb_ref, s0_ref, dmat_ref, qsel_ref, amask_ref,
         o_ref, st_scr) = refs
    t = pl.program_id(2)

    @pl.when(t == 0)
    def _():
        st_scr[...] = s0_ref[0, 0]

    order = list(range(nchunk - 1, -1, -1) if reverse else range(nchunk))
    out, st = _hg_block(q_ref[0], z_ref[0], v_ref[0], lb_ref[0], st_scr[...], dmat_ref[...],
                        qsel_ref, amask_ref, order, reverse)
    o = jnp.concatenate([out[ci] for ci in range(nchunk)], axis=0)
    if fuse_readout:
        o = o + of_ref[0]
        on = o * lax.rsqrt(jnp.mean(o * o, axis=-1, keepdims=True) + EPS) * gn_ref[...]
        g = og_ref[0].astype(F32)
        o_ref[0] = (on * (g * jax.nn.sigmoid(g))).astype(o_ref.dtype)
    else:
        o_ref[0] = o
    st_scr[...] = st


def _hg_state_kernel(reverse, nchunk, v_ref, z_ref, lb_ref, pmat_ref, sout_ref):
    c = HG_CHUNK
    v = v_ref[0]
    f = lb_ref[...] + (1.0 - lb_ref[...]) * jax.nn.sigmoid(z_ref[0])
    kk = 1.0 - f
    pieces = _split3(jnp.log(f))
    order = list(range(nchunk - 1, -1, -1) if reverse else range(nchunk))
    last = 0 if reverse else c - 1
    tile = lambda a, h, ci: a[ci * c:(ci + 1) * c, h * HG_DIM:(h + 1) * HG_DIM]
    pre, upd = {}, {}
    for h in range(HG_HEADS):
        head = [a[:, h * HG_DIM:(h + 1) * HG_DIM] for a in pieces]
        for ci, p in _chunk_prefixes(pmat_ref[...], head, order).items():
            pre[h, ci] = p
    for k, p in pre.items():
        e_out = jnp.exp(p[last:last + 1] - p)
        upd[k] = lax.dot_general(tile(v, *k), (tile(kk, *k) * e_out).astype(BF16), (((0,), (0,)), ((), ())),
                                 preferred_element_type=F32)
    for h in range(HG_HEADS):
        st = jnp.zeros((HG_DIM, HG_DIM), F32)
        for ci in order:
            st = st * jnp.exp(pre[h, ci][last:last + 1]) + upd[h, ci]
        sout_ref[0, h] = st


def _hgrn_state(val, fg, lb, dmat, reverse):
    b, l, _ = val.shape
    assert l % HG_CHUNK == 0
    fblk = 1 if reverse else 0
    return pl.pallas_call(
        functools.partial(_hg_state_kernel, reverse, l // HG_CHUNK),
        grid=(b,),
        in_specs=[pl.BlockSpec((1, l, HG_WIDTH), lambda bi: (bi, 0, 0)),
                  pl.BlockSpec((1, l, HG_WIDTH), lambda bi: (bi, 0, fblk)),
                  pl.BlockSpec((1, HG_WIDTH), lambda bi: (0, 0)),
                  pl.BlockSpec(dmat.shape, lambda bi: (0, 0))],
        out_specs=pl.BlockSpec((1, HG_HEADS, HG_DIM, HG_DIM), lambda bi: (bi, 0, 0, 0)),
        out_shape=jax.ShapeDtypeStruct((b, HG_HEADS, HG_DIM, HG_DIM), F32),
        compiler_params=_cparams(("parallel",)),
        name="hgrn_state_bwd" if reverse else "hgrn_state_fwd",
    )(val, fg, lb.reshape(1, HG_WIDTH), dmat)


def _hgrn(mix, fg, lb, s0, consts, reverse, d, o_fwd=None, gn=None):
    b, s, _ = mix.shape
    tt = _pick(s, (2048, 1024, 512, 256, 128, 64))
    nt = s // tt
    base = 2 * d // HG_DIM
    fuse = o_fwd is not None
    dmat, qsel, amask = consts
    fcol = HG_HEADS if reverse else 0

    def tmap(t):
        return nt - 1 - t if reverse else t

    blk = (1, tt, HG_DIM)
    in_specs = [pl.BlockSpec(blk, lambda bi, h, t: (bi, tmap(t), base + h)),
                pl.BlockSpec(blk, lambda bi, h, t: (bi, tmap(t), base + HG_HEADS + h)),
                pl.BlockSpec(blk, lambda bi, h, t: (bi, tmap(t), fcol + h)),
                pl.BlockSpec((1, 1, HG_DIM), lambda bi, h, t: (h, 0, 0)),
                pl.BlockSpec((1, 1, HG_DIM, HG_DIM), lambda bi, h, t: (bi, h, 0, 0)),
                pl.BlockSpec(dmat.shape, lambda bi, h, t: (0, 0)),
                pl.BlockSpec(qsel.shape, lambda bi, h, t: (0, 0, 0)),
                pl.BlockSpec(amask.shape, lambda bi, h, t: (0, 0, 0))]
    args = [mix, mix, fg, lb.reshape(HG_HEADS, 1, HG_DIM), s0, dmat, qsel, amask]
    if fuse:
        in_specs += [pl.BlockSpec(blk, lambda bi, h, t: (bi, tmap(t), h)),
                     pl.BlockSpec(blk, lambda bi, h, t: (bi, tmap(t), base + 2 * HG_HEADS + h)),
                     pl.BlockSpec((1, HG_DIM), lambda bi, h, t: (0, 0))]
        args += [o_fwd, mix, gn.reshape(1, HG_DIM)]
    return pl.pallas_call(
        functools.partial(_hg_kernel, reverse, fuse, tt // HG_CHUNK),
        grid=(b, HG_HEADS, nt),
        in_specs=in_specs,
        out_specs=pl.BlockSpec(blk, lambda bi, h, t: (bi, tmap(t), h)),
        out_shape=jax.ShapeDtypeStruct((b, s, HG_WIDTH), BF16 if fuse else F32),
        scratch_shapes=[pltpu.VMEM((HG_DIM, HG_DIM), F32)],
        compiler_params=_cparams(("parallel", "parallel", "arbitrary")),
        name="hgrn_bwd" if reverse else "hgrn_fwd",
    )(*args)


def _merge_kernel(oa_ref, ob_ref, ga_ref, gb_ref, x_ref, g1_ref, n2_ref, sh2_ref, sc2_ref,
                  wpa_ref, wpb_ref, wo_ref, x1_ref, h2_ref, y_scr):
    tm, d = x1_ref.shape[1:]
    nc = _pick(d, (512, 256, 128))
    oa, ob = oa_ref[0], ob_ref[0]
    for n0 in range(0, d, nc):
        cs = slice(n0, n0 + nc)
        ya = jnp.dot(oa, wpa_ref[:, cs], preferred_element_type=F32)
        yb = jnp.dot(ob, wpb_ref[:, cs], preferred_element_type=F32)
        y_scr[:, cs] = (jax.nn.sigmoid(ga_ref[0, :, cs].astype(F32)) * ya
                        + jax.nn.sigmoid(gb_ref[0, :, cs].astype(F32)) * yb).astype(BF16)
    y = y_scr[...]
    ss = jnp.zeros((tm, 1), F32)
    for n0 in range(0, d, nc):
        cs = slice(n0, n0 + nc)
        x1 = x_ref[0, :, cs] + g1_ref[0, :, cs] * jnp.dot(y, wo_ref[:, cs], preferred_element_type=F32)
        x1_ref[0, :, cs] = x1
        ss = ss + jnp.sum(x1 * x1, axis=-1, keepdims=True)
    r = lax.rsqrt(ss * (1.0 / d) + EPS)
    h2_ref[0] = (x1_ref[0] * r * n2_ref[...] * (1.0 + sc2_ref[0]) + sh2_ref[0]).astype(BF16)


def _merge(o_a, o_b, mix, x, mod, n2, wpa, wpb, wo):
    b, s, d = x.shape
    tm = _pick(s, (512, 256, 128))
    const = lambda shape: pl.BlockSpec(shape, lambda bi, i: (0,) * len(shape), pipeline_mode=pl.Buffered(1))
    vec = lambda j: _mod_spec(d, 2, lambda bi: bi, j)
    return pl.pallas_call(
        _merge_kernel,
        grid=(b, s // tm),
        in_specs=[pl.BlockSpec((1, tm, NA_WIDTH), lambda bi, i: (bi, i, 0)),
                  pl.BlockSpec((1, tm, HG_WIDTH), lambda bi, i: (bi, i, 0)),
                  pl.BlockSpec((1, tm, d), lambda bi, i: (bi, i, 0)),
                  pl.BlockSpec((1, tm, d), lambda bi, i: (bi, i, 1)),
                  pl.BlockSpec((1, tm, d), lambda bi, i: (bi, i, 0)),
                  vec(2), const((1, d)), vec(3), vec(4),
                  const(wpa.shape), const(wpb.shape), const(wo.shape)],
        out_specs=[pl.BlockSpec((1, tm, d), lambda bi, i: (bi, i, 0)),
                   pl.BlockSpec((1, tm, d), lambda bi, i: (bi, i, 0))],
        out_shape=[jax.ShapeDtypeStruct((b, s, d), F32), jax.ShapeDtypeStruct((b, s, d), BF16)],
        scratch_shapes=[pltpu.VMEM((tm, d), BF16)],
        compiler_params=_cparams(("parallel", "parallel")),
        name="merge",
    )(o_a, o_b, mix, mix, x, mod, n2, mod, mod, wpa, wpb, wo)


def _ffn_kernel(nf, nx, h_ref, x1c_ref, g2_ref, fg_ref, wa_ref, wu_ref, wo_ref, o_ref, x1_scr):
    j = pl.program_id(2)
    tm, d = o_ref.shape[1:]
    rows = min(tm, MXU_ROWS_PER_DOT)
    nc = _pick(d, (512, 256, 128))
    xr = tm // nx

    @pl.when(j < nx)
    def _():
        x1_scr[pl.ds(pl.multiple_of(j * xr, xr), xr), :] = x1c_ref[0]

    def hidden_slice(first, last):
        for r0 in range(0, tm, rows):
            rs = slice(r0, r0 + rows)
            h = h_ref[0, rs]
            a = jnp.dot(h, wa_ref[...], preferred_element_type=F32)
            u = jnp.dot(h, wu_ref[...], preferred_element_type=F32)
            gate = (a * jax.nn.sigmoid(a) * u).astype(BF16)
            for n0 in range(0, d, nc):
                part = jnp.dot(gate, wo_ref[:, n0:n0 + nc], preferred_element_type=F32)
                if first:
                    o_ref[0, rs, n0:n0 + nc] = part
                else:
                    o_ref[0, rs, n0:n0 + nc] += part
            if last:
                x2 = x1_scr[rs, :] + g2_ref[0] * o_ref[0, rs]
                ms = jnp.mean(x2 * x2, axis=-1, keepdims=True)
                o_ref[0, rs] = x2 * lax.rsqrt(ms + EPS) * fg_ref[...]

    if nf == 1:
        hidden_slice(True, True)
    else:
        pl.when(j == 0)(functools.partial(hidden_slice, True, False))
        if nf > 2:
            pl.when((j > 0) & (j < nf - 1))(functools.partial(hidden_slice, False, False))
        pl.when(j == nf - 1)(functools.partial(hidden_slice, False, True))


def _ffn(h2, x1, mod, final_g, w_in, w_out):
    b, s, d = x1.shape
    fh = w_out.shape[0]
    tf = _pick(fh, (512, 256, 128))
    nf = fh // tf
    tm = _pick(s, (1024, 512, 256, 128))
    nx = max(n for n in (1, 2, 4, 8) if n <= nf and tm % (8 * n) == 0)
    return pl.pallas_call(
        functools.partial(_ffn_kernel, nf, nx),
        grid=(b, s // tm, nf),
        in_specs=[pl.BlockSpec((1, tm, d), lambda bi, i, j: (bi, i, 0)),
                  pl.BlockSpec((1, tm // nx, d), lambda bi, i, j: (bi, i * nx + jnp.minimum(j, nx - 1), 0)),
                  _mod_spec(d, 3, lambda bi: bi, 5),
                  pl.BlockSpec((1, d), lambda bi, i, j: (0, 0)),
                  pl.BlockSpec((d, tf), lambda bi, i, j: (0, j)),
                  pl.BlockSpec((d, tf), lambda bi, i, j: (0, nf + j)),
                  pl.BlockSpec((tf, d), lambda bi, i, j: (j, 0))],
        out_specs=pl.BlockSpec((1, tm, d), lambda bi, i, j: (bi, i, 0)),
        out_shape=jax.ShapeDtypeStruct((b, s, d), F32),
        scratch_shapes=[pltpu.VMEM((tm, d), F32)],
        compiler_params=_cparams(("parallel", "parallel", "arbitrary")),
        name="ffn",
    )(h2, x1, mod, final_g.reshape(1, d), w_in, w_in, w_out)


def kernel(x, c, ctx, c_ctx, w_ada, b_ada, norm1_g, w_in, na_rpb, hg_lb_logits, hg_norm_g,
           w_pa, w_pb, w_out, norm2_g, w_ffn_in, w_ffn_out, final_g):
    b, s, d = x.shape
    assert w_ada.shape[0] == 1, "single layer"

    lb_table = jnp.cumsum(jax.nn.softmax(hg_lb_logits.astype(F32), axis=0), axis=0)
    lb_f, lb_b = lb_table[0, 0], lb_table[0, 1]

    c8 = jnp.zeros((8, d), F32).at[:b].set(c).at[b].set(c_ctx)
    mod = _ada(c8, w_ada[0], b_ada[0]).reshape(8 * N_MOD, 1, d)

    assert NA_WIDTH == HG_WIDTH == INPROJ_TN and (2 * d) % INPROJ_TN == 0
    q_, k_, v_, hq_, ff_, fb_, hi_, hog_ = range(8)
    gates = list(range(8, 8 + 2 * d // INPROJ_TN))
    w_bf = w_in[0].astype(BF16)
    n1 = norm1_g[0].reshape(1, d)

    qkv, mix, fg = _norm_inproj(x, n1, mod, lambda bi: bi, w_bf,
                                [(BF16, [q_, k_, v_]), (BF16, gates + [hq_, hi_, hog_]), (F32, [ff_, fb_])])
    kv_c, hi_c, fg_c = _norm_inproj(ctx, n1, mod, lambda bi: b, w_bf,
                                    [(BF16, [k_, v_]), (BF16, [hi_]), (F32, [ff_, fb_])])

    o_na = _na(qkv, kv_c, _na_bias_table(na_rpb[0]))

    cf, cb = _hg_constants(False), _hg_constants(True)
    s_f = _hgrn_state(hi_c, fg_c, lb_f, cf[0], False)
    s_b = _hgrn_state(hi_c, fg_c, lb_b, cb[0], True)
    o_f = _hgrn(mix, fg, lb_f, s_f, cf, False, d)
    o_hg = _hgrn(mix, fg, lb_b, s_b, cb, True, d, o_fwd=o_f, gn=hg_norm_g[0])

    x1, h2 = _merge(o_na, o_hg, mix, x, mod, norm2_g[0].reshape(1, d),
                    w_pa[0].astype(BF16), w_pb[0].astype(BF16), w_out[0].astype(BF16))
    return _ffn(h2, x1, mod, final_g, w_ffn_in[0].astype(BF16), w_ffn_out[0].astype(BF16))
```

```python
import functools

import jax
import jax.numpy as jnp
import numpy as np
from jax import lax
from jax.experimental import pallas as pl
from jax.experimental.pallas import tpu as pltpu

F32 = jnp.float32
BF16 = jnp.bfloat16

GRID_W = 64
WIN_H = 8
WIN_W = 16
NA_HEADS = 16
NA_HEAD_DIM = 64
NA_WIDTH = NA_HEADS * NA_HEAD_DIM
HG_HEADS = 8
HG_DIM = 128
HG_WIDTH = HG_HEADS * HG_DIM
N_MOD = 6
EPS = 1e-6
NEG_BIG = -1e30

V7X_VMEM_LIMIT_BYTES = 60000 * 1024
MXU_ROWS_PER_DOT = 512
NA_SLAB_HEADS = 4
NA_ROWS_PER_STEP = 2
HG_CHUNK = 64
HG_LEVELS = 6
HG_VPU_LEVELS = 1
LOG2_E = 1.4426950408889634


def _cparams(sem):
    return pltpu.CompilerParams(dimension_semantics=sem, vmem_limit_bytes=V7X_VMEM_LIMIT_BYTES)


def _pick(n, prefs):
    for p in prefs:
        if n % p == 0:
            return p
    return n


def _ada_kernel(c_ref, w_ref, b_ref, o_ref):
    c = c_ref[...]
    s = c * jax.nn.sigmoid(c)
    o_ref[...] = jnp.dot(s, w_ref[...], precision=lax.Precision.HIGHEST,
                         preferred_element_type=F32) + b_ref[...]


def _ada(c8, w, b):
    d, n = w.shape
    tn = _pick(n, (1024, 512, 256, 128))
    return pl.pallas_call(
        _ada_kernel,
        grid=(n // tn,),
        in_specs=[pl.BlockSpec((8, d), lambda j: (0, 0)),
                  pl.BlockSpec((d, tn), lambda j: (0, j)),
                  pl.BlockSpec((1, tn), lambda j: (0, j))],
        out_specs=pl.BlockSpec((8, tn), lambda j: (0, j)),
        out_shape=jax.ShapeDtypeStruct((8, n), F32),
        compiler_params=_cparams(("arbitrary",)),
        name="ada",
    )(c8, w, b.reshape(1, n))


INPROJ_TN = 1024
NORM_CHUNKS = (8, 4, 2, 1)


def _norm_inproj_kernel(starts, nx, x_ref, g_ref, sh_ref, sc_ref, w_ref, *refs):
    o_refs, h_bufs = refs[:-2], refs[-2:]
    i, j = pl.program_id(1), pl.program_id(2)
    tm = h_bufs[0].shape[0]
    xr = tm // nx
    rows = min(tm, MXU_ROWS_PER_DOT)

    def normalise_chunk(h_fill):
        x = x_ref[0]
        ms = jnp.mean(x * x, axis=-1, keepdims=True)
        y = x * lax.rsqrt(ms + EPS) * g_ref[...]
        r0 = pl.multiple_of(jnp.minimum(j, nx - 1) * xr, xr)
        h_fill[pl.ds(r0, xr), :] = (y * (1.0 + sc_ref[0]) + sh_ref[0]).astype(BF16)

    def project(o_ref, h_fill, h_use):
        normalise_chunk(h_fill)
        for r0 in range(0, tm, rows):
            o_ref[0, r0:r0 + rows] = jnp.dot(h_use[r0:r0 + rows], w_ref[...],
                                             preferred_element_type=F32).astype(o_ref.dtype)

    pl.when(i == 0)(functools.partial(normalise_chunk, h_bufs[0]))
    for parity in (0, 1):
        for k, o_ref in enumerate(o_refs):
            pl.when((i > 0) & (i % 2 == parity) & (j >= starts[k]) & (j < starts[k + 1]))(
                functools.partial(project, o_ref, h_bufs[parity], h_bufs[1 - parity]))


def _mod_spec(d, grid_rank, row, j):
    if grid_rank == 2:
        return pl.BlockSpec((1, 1, d), lambda bi, i: (row(bi) * N_MOD + j, 0, 0))
    return pl.BlockSpec((1, 1, d), lambda bi, i, k: (row(bi) * N_MOD + j, 0, 0))


def _norm_inproj(x, g, mod, row, w, outputs):
    b, t, d = x.shape
    tn = INPROJ_TN
    tm = _pick(t, (1024, 512, 256))
    ni = t // tm
    counts = [len(blocks) for _, blocks in outputs]
    starts = [int(v) for v in np.cumsum([0] + counts)]
    nj = starts[-1]
    table = [blk for _, blocks in outputs for blk in blocks]
    nx = next(n for n in NORM_CHUNKS if n <= nj and tm % (16 * n) == 0)
    xr = tm // nx

    def w_block(i, j):
        r = jnp.int32(table[-1])
        for idx in range(len(table) - 2, -1, -1):
            r = jnp.where(j == idx, table[idx], r)
        return jnp.where(i == 0, table[0], r)

    def x_block(i, j):
        return jnp.where(i >= ni, ni * nx - 1, i * nx + jnp.minimum(j, nx - 1))

    def o_spec(k):
        return pl.BlockSpec((1, tm, tn), lambda bi, i, j: (
            bi, jnp.maximum(i - 1, 0), jnp.where(i == 0, 0, jnp.clip(j - starts[k], 0, counts[k] - 1))))

    return pl.pallas_call(
        functools.partial(_norm_inproj_kernel, starts, nx),
        grid=(b, ni + 1, nj),
        in_specs=[pl.BlockSpec((1, xr, d), lambda bi, i, j: (bi, x_block(i, j), 0)),
                  pl.BlockSpec((1, d), lambda bi, i, j: (0, 0)),
                  _mod_spec(d, 3, row, 0), _mod_spec(d, 3, row, 1),
                  pl.BlockSpec((d, tn), lambda bi, i, j: (0, w_block(i, j)))],
        out_specs=[o_spec(k) for k in range(len(outputs))],
        out_shape=[jax.ShapeDtypeStruct((b, t, n * tn), dt) for (dt, _), n in zip(outputs, counts)],
        scratch_shapes=[pltpu.VMEM((tm, d), BF16), pltpu.VMEM((tm, d), BF16)],
        compiler_params=_cparams(("parallel", "arbitrary", "arbitrary")),
        name="norm_inproj",
    )(x, g, mod, mod, w)


def _na_kernel(nb, nr, q_ref, *refs):
    kb_refs, vb_refs, (kc_ref, vc_ref), bias_refs, o_ref = (refs[:nr], refs[nr:2 * nr], refs[2 * nr:2 * nr + 2],
                                                            refs[2 * nr + 2:3 * nr + 2], refs[3 * nr + 2])
    dn = (((1,), (1,)), ((), ()))
    nh, w = NA_SLAB_HEADS, NA_SLAB_HEADS * NA_HEAD_DIM
    hq = nh * GRID_W
    lane_head = lax.broadcasted_iota(jnp.int32, (GRID_W, w), 1) // NA_HEAD_DIM
    own = [lane_head == a for a in range(nh)]
    slabs = range(NA_HEADS // nh)
    work = [(j, g) for j in range(nr) for g in slabs]
    cols = {g: slice(g * w, (g + 1) * w) for g in slabs}
    q = {}
    for j, g in work:
        qg = q_ref[0, j * GRID_W:(j + 1) * GRID_W, cols[g]] * (NA_HEAD_DIM ** -0.5)
        q[j, g] = jnp.concatenate([jnp.where(own[a], qg, jnp.zeros_like(qg)) for a in range(nh)], axis=0)
    s_b = {(j, g): lax.dot_general(q[j, g], kb_refs[j][0, :, cols[g]], dn, preferred_element_type=F32)
           + bias_refs[j][0, g * nh:(g + 1) * nh].reshape(hq, nb) for j, g in work}
    s_c = {}
    for g in slabs:
        both = lax.dot_general(jnp.concatenate([q[j, g] for j in range(nr)], axis=0), kc_ref[0, :, cols[g]], dn,
                               preferred_element_type=F32)
        for j in range(nr):
            s_c[j, g] = both[j * hq:(j + 1) * hq]
    m = {k: jnp.maximum(jnp.max(s_b[k], axis=-1, keepdims=True), jnp.max(s_c[k], axis=-1, keepdims=True))
         for k in work}
    p_b = {k: jnp.exp(s_b[k] - m[k]) for k in work}
    p_c = {k: jnp.exp(s_c[k] - m[k]) for k in work}
    l = {k: jnp.sum(p_b[k], axis=-1, keepdims=True) + jnp.sum(p_c[k], axis=-1, keepdims=True) for k in work}
    o_c = {}
    for g in slabs:
        both = jnp.dot(jnp.concatenate([p_c[j, g] for j in range(nr)], axis=0).astype(BF16), vc_ref[0, :, cols[g]],
                       preferred_element_type=F32)
        for j in range(nr):
            o_c[j, g] = both[j * hq:(j + 1) * hq]
    o = {(j, g): (jnp.dot(p_b[j, g].astype(BF16), vb_refs[j][0, :, cols[g]], preferred_element_type=F32)
                  + o_c[j, g]) / l[j, g] for j, g in work}
    for j, g in work:
        out = o[j, g][0:GRID_W]
        for a in range(1, nh):
            out = jnp.where(own[a], o[j, g][a * GRID_W:(a + 1) * GRID_W], out)
        o_ref[0, j * GRID_W:(j + 1) * GRID_W, cols[g]] = out.astype(BF16)


def _na_bias_table(rpb):
    col = np.arange(GRID_W)
    col_start = np.clip(col - WIN_W // 2, 0, GRID_W - WIN_W)
    in_win = (col[None, :] >= col_start[:, None]) & (col[None, :] < col_start[:, None] + WIN_W)
    dc_idx = np.clip(col[None, :] - col[:, None], 1 - WIN_W, WIN_W - 1) + WIN_W - 1
    onehot = (dc_idx[None] == np.arange(2 * WIN_W - 1)[:, None, None]).astype(np.float32)
    g = jnp.einsum('hdc,cqk->hdqk', rpb.astype(F32), jnp.asarray(onehot),
                   precision=lax.Precision.HIGHEST)
    g = jnp.where(jnp.asarray(in_win)[None, None], g, NEG_BIG)
    tabs = [g[:, off:off + WIN_H].transpose(0, 2, 1, 3).reshape(NA_HEADS, GRID_W, WIN_H * GRID_W)
            for off in range(WIN_H)]
    return jnp.stack(tabs)


def _na(qkv, kv_c, bias_tab):
    b, s, _ = qkv.shape
    lc = kv_c.shape[1]
    rows = s // GRID_W
    nr = _pick(rows, (NA_ROWS_PER_STEP, 1))
    assert rows >= WIN_H
    nb = WIN_H * GRID_W

    def rs_of(r):
        return jnp.clip(r - WIN_H // 2, 0, rows - WIN_H)

    band = (pl.Element(1), pl.Element(nb), pl.Element(NA_WIDTH))
    k_band = [pl.BlockSpec(band, lambda bi, rr, j=j: (bi, rs_of(rr * nr + j) * GRID_W, NA_WIDTH))
              for j in range(nr)]
    v_band = [pl.BlockSpec(band, lambda bi, rr, j=j: (bi, rs_of(rr * nr + j) * GRID_W, 2 * NA_WIDTH))
              for j in range(nr)]
    bias = [pl.BlockSpec((1, NA_HEADS, GRID_W, nb),
                         lambda bi, rr, j=j: (rs_of(rr * nr + j) - (rr * nr + j) + WIN_H - 1, 0, 0, 0))
            for j in range(nr)]
    return pl.pallas_call(
        functools.partial(_na_kernel, nb, nr),
        grid=(b, rows // nr),
        in_specs=[pl.BlockSpec((1, nr * GRID_W, NA_WIDTH), lambda bi, rr: (bi, rr, 0)), *k_band, *v_band,
                  pl.BlockSpec((1, lc, NA_WIDTH), lambda bi, rr: (bi, 0, 0)),
                  pl.BlockSpec((1, lc, NA_WIDTH), lambda bi, rr: (bi, 0, 1)), *bias],
        out_specs=pl.BlockSpec((1, nr * GRID_W, NA_WIDTH), lambda bi, rr: (bi, rr, 0)),
        out_shape=jax.ShapeDtypeStruct((b, s, NA_WIDTH), BF16),
        compiler_params=_cparams(("parallel", "arbitrary")),
        name="na",
    )(qkv, *([qkv] * (2 * nr)), kv_c, kv_c, *([bias_tab] * nr))


def _hg_constants(reverse):
    c = HG_CHUNK
    pos = np.arange(c)
    pmat = np.tril(np.ones((c, c), np.float32))
    qsel = np.zeros((HG_LEVELS, c, HG_DIM), np.float32)
    amask = np.zeros((HG_LEVELS - 1, c, c), np.float32)
    for l in range(HG_LEVELS):
        m = 1 << l
        blk = pos // (2 * m)
        is_q = (pos % (2 * m)) >= m
        qsel[l] = is_q[:, None]
        if l > 0:
            amask[l - 1] = (blk[:, None] == blk[None, :]) & is_q[:, None] & ~is_q[None, :]
    if reverse:
        pmat = pmat[::-1, ::-1]
        qsel = qsel[:, ::-1]
        amask = amask[:, ::-1, ::-1]
    return (jnp.asarray(np.concatenate([pmat, pmat, pmat], axis=1), BF16),
            jnp.asarray(qsel, F32), jnp.asarray(amask, F32))


def _boundary_rows(p, l, reverse):
    c = HG_CHUNK
    m = 1 << l
    off = m if reverse else m - 1
    if 2 * m >= 16:
        return jnp.concatenate([jnp.broadcast_to(p[b0 + off:b0 + off + 1], (2 * m, HG_DIM))
                                for b0 in range(0, c, 2 * m)], axis=0)
    p3 = p.reshape(c // 8, 8, HG_DIM)
    pick = lambda s: jnp.broadcast_to(p3[:, s:s + 1], p3.shape)
    if 2 * m == 8:
        return pick(off).reshape(c, HG_DIM)
    assert 2 * m == 4
    upper = lax.broadcasted_iota(jnp.int32, p3.shape, 1) >= 4
    return jnp.where(upper, pick(4 + off), pick(off)).reshape(c, HG_DIM)


def _chunk_prefixes(pmat, pieces, order):
    c = HG_CHUNK
    rows = lambda a, ci: a[ci * c:(ci + 1) * c]
    pre = {}
    for ca, cb in zip(order[0::2], order[1::2]):
        rhs = jnp.concatenate([jnp.concatenate([rows(a, ca), rows(a, cb)], axis=1) for a in pieces], axis=0)
        both = jnp.dot(pmat, rhs, preferred_element_type=F32)
        pre[ca], pre[cb] = both[:, :HG_DIM], both[:, HG_DIM:]
    if len(order) % 2:
        ci = order[-1]
        pre[ci] = jnp.dot(pmat, jnp.concatenate([rows(a, ci) for a in pieces], axis=0),
                          preferred_element_type=F32)
    return pre


def _split3(a):
    hi = a.astype(BF16)
    r1 = a - hi.astype(F32)
    mid = r1.astype(BF16)
    lo = (r1 - mid.astype(F32)).astype(BF16)
    return hi, mid, lo


def _hg_block(q, z, v, lb, st, pmat, qsel_ref, amask_ref, order, reverse):
    c = HG_CHUNK
    dn_nt = (((1,), (1,)), ((), ()))
    f = lb + (1.0 - lb) * jax.nn.sigmoid(z)
    kk = 1.0 - f
    qf = q.astype(F32)
    rows = lambda a, ci: a[ci * c:(ci + 1) * c]
    pre = _chunk_prefixes(pmat, _split3(jnp.log(f)), order)
    qsel = [qsel_ref[l] != 0.0 for l in range(HG_LEVELS)]
    amask = [amask_ref[l] != 0.0 for l in range(HG_LEVELS - 1)]
    a = {}
    for l in range(HG_VPU_LEVELS, HG_LEVELS):
        sign = jnp.where(qsel[l], LOG2_E, -LOG2_E)
        w = {ci: (jnp.exp2((pre[ci] - _boundary_rows(pre[ci], l, reverse)) * sign)
                  * jnp.where(qsel[l], rows(qf, ci), rows(kk, ci))).astype(BF16) for ci in order}
        p = {ci: lax.dot_general(w[ci], w[ci], dn_nt, preferred_element_type=F32) for ci in order}
        a = {ci: jnp.where(amask[l - 1], p[ci], a[ci] if l > HG_VPU_LEVELS else 0.0) for ci in order}
    last = 0 if reverse else c - 1
    e_in = {ci: jnp.exp(pre[ci]) for ci in order}
    e_out = {ci: jnp.exp(pre[ci][last:last + 1] - pre[ci]) for ci in order}
    upd = {ci: lax.dot_general(rows(v, ci), (rows(kk, ci) * e_out[ci]).astype(BF16), (((0,), (0,)), ((), ())),
                               preferred_element_type=F32) for ci in order}
    o_intra = {ci: jnp.dot(a[ci].astype(BF16), rows(v, ci), preferred_element_type=F32) for ci in order}
    vf = v.astype(F32)
    small = 1 << HG_VPU_LEVELS
    row = lax.broadcasted_iota(jnp.int32, (c, 1), 0)
    in_block = ((c - 1 - row) if reverse else row) % small
    back = lambda x, dist: pltpu.roll(x, (c - dist) if reverse else dist, 0)
    for ci in order:
        qc, kc, vc, fc = rows(qf, ci), rows(kk, ci), rows(vf, ci), rows(f, ci)
        acc = o_intra[ci] + jnp.sum(qc * kc, axis=-1, keepdims=True) * vc
        decay = fc
        for dist in range(1, small):
            if dist > 1:
                decay = decay * back(fc, dist - 1)
            wgt = jnp.sum(qc * decay * back(kc, dist), axis=-1, keepdims=True)
            acc = acc + jnp.where(in_block >= dist, wgt, 0.0) * back(vc, dist)
        o_intra[ci] = acc
    qd = {ci: (rows(qf, ci) * e_in[ci]).astype(BF16) for ci in order}
    out = {}
    for ci in order:
        out[ci] = o_intra[ci] + lax.dot_general(qd[ci], st.astype(BF16), dn_nt, preferred_element_type=F32)
        st = st * e_in[ci][last:last + 1] + upd[ci]
    return out, st


def _hg_kernel(reverse, fuse_readout, nchunk, *refs):
    if fuse_readout:
        (q_ref, v_ref, z_ref, lb_ref, s0_ref, dmat_ref, qsel_ref, amask_ref, of_ref, og_ref, gn_ref,
         o_ref, st_scr) = refs
    else:
        (q_ref, v_ref, z_ref, lb_ref, s0_ref, dmat_ref, qsel_ref, amask_ref,
         o_ref, st_scr) = refs
    t = pl.program_id(2)

    @pl.when(t == 0)
    def _():
        st_scr[...] = s0_ref[0, 0]

    order = list(range(nchunk - 1, -1, -1) if reverse else range(nchunk))
    out, st = _hg_block(q_ref[0], z_ref[0], v_ref[0], lb_ref[0], st_scr[...], dmat_ref[...],
                        qsel_ref, amask_ref, order, reverse)
    o = jnp.concatenate([out[ci] for ci in range(nchunk)], axis=0)
    if fuse_readout:
        o = o + of_ref[0]
        on = o * lax.rsqrt(jnp.mean(o * o, axis=-1, keepdims=True) + EPS) * gn_ref[...]
        g = og_ref[0].astype(F32)
        o_ref[0] = (on * (g * jax.nn.sigmoid(g))).astype(o_ref.dtype)
    else:
        o_ref[0] = o
    st_scr[...] = st


def _hg_state_kernel(reverse, nchunk, v_ref, z_ref, lb_ref, pmat_ref, sout_ref):
    c = HG_CHUNK
    v = v_ref[0]
    f = lb_ref[...] + (1.0 - lb_ref[...]) * jax.nn.sigmoid(z_ref[0])
    kk = 1.0 - f
    pieces = _split3(jnp.log(f))
    order = list(range(nchunk - 1, -1, -1) if reverse else range(nchunk))
    last = 0 if reverse else c - 1
    tile = lambda a, h, ci: a[ci * c:(ci + 1) * c, h * HG_DIM:(h + 1) * HG_DIM]
    pre, upd = {}, {}
    for h in range(HG_HEADS):
        head = [a[:, h * HG_DIM:(h + 1) * HG_DIM] for a in pieces]
        for ci, p in _chunk_prefixes(pmat_ref[...], head, order).items():
            pre[h, ci] = p
    for k, p in pre.items():
        e_out = jnp.exp(p[last:last + 1] - p)
        upd[k] = lax.dot_general(tile(v, *k), (tile(kk, *k) * e_out).astype(BF16), (((0,), (0,)), ((), ())),
                                 preferred_element_type=F32)
    for h in range(HG_HEADS):
        st = jnp.zeros((HG_DIM, HG_DIM), F32)
        for ci in order:
            st = st * jnp.exp(pre[h, ci][last:last + 1]) + upd[h, ci]
        sout_ref[0, h] = st


def _hgrn_state(val, fg, lb, dmat, reverse):
    b, l, _ = val.shape
    assert l % HG_CHUNK == 0
    fblk = 1 if reverse else 0
    return pl.pallas_call(
        functools.partial(_hg_state_kernel, reverse, l // HG_CHUNK),
        grid=(b,),
        in_specs=[pl.BlockSpec((1, l, HG_WIDTH), lambda bi: (bi, 0, 0)),
                  pl.BlockSpec((1, l, HG_WIDTH), lambda bi: (bi, 0, fblk)),
                  pl.BlockSpec((1, HG_WIDTH), lambda bi: (0, 0)),
                  pl.BlockSpec(dmat.shape, lambda bi: (0, 0))],
        out_specs=pl.BlockSpec((1, HG_HEADS, HG_DIM, HG_DIM), lambda bi: (bi, 0, 0, 0)),
        out_shape=jax.ShapeDtypeStruct((b, HG_HEADS, HG_DIM, HG_DIM), F32),
        compiler_params=_cparams(("parallel",)),
        name="hgrn_state_bwd" if reverse else "hgrn_state_fwd",
    )(val, fg, lb.reshape(1, HG_WIDTH), dmat)


def _hgrn(mix, fg, lb, s0, consts, reverse, d, o_fwd=None, gn=None):
    b, s, _ = mix.shape
    tt = _pick(s, (2048, 1024, 512, 256, 128, 64))
    nt = s // tt
    base = 2 * d // HG_DIM
    fuse = o_fwd is not None
    dmat, qsel, amask = consts
    fcol = HG_HEADS if reverse else 0

    def tmap(t):
        return nt - 1 - t if reverse else t

    blk = (1, tt, HG_DIM)
    in_specs = [pl.BlockSpec(blk, lambda bi, h, t: (bi, tmap(t), base + h)),
                pl.BlockSpec(blk, lambda bi, h, t: (bi, tmap(t), base + HG_HEADS + h)),
                pl.BlockSpec(blk, lambda bi, h, t: (bi, tmap(t), fcol + h)),
                pl.BlockSpec((1, 1, HG_DIM), lambda bi, h, t: (h, 0, 0)),
                pl.BlockSpec((1, 1, HG_DIM, HG_DIM), lambda bi, h, t: (bi, h, 0, 0)),
                pl.BlockSpec(dmat.shape, lambda bi, h, t: (0, 0)),
                pl.BlockSpec(qsel.shape, lambda bi, h, t: (0, 0, 0)),
                pl.BlockSpec(amask.shape, lambda bi, h, t: (0, 0, 0))]
    args = [mix, mix, fg, lb.reshape(HG_HEADS, 1, HG_DIM), s0, dmat, qsel, amask]
    if fuse:
        in_specs += [pl.BlockSpec(blk, lambda bi, h, t: (bi, tmap(t), h)),
                     pl.BlockSpec(blk, lambda bi, h, t: (bi, tmap(t), base + 2 * HG_HEADS + h)),
                     pl.BlockSpec((1, HG_DIM), lambda bi, h, t: (0, 0))]
        args += [o_fwd, mix, gn.reshape(1, HG_DIM)]
    return pl.pallas_call(
        functools.partial(_hg_kernel, reverse, fuse, tt // HG_CHUNK),
        grid=(b, HG_HEADS, nt),
        in_specs=in_specs,
        out_specs=pl.BlockSpec(blk, lambda bi, h, t: (bi, tmap(t), h)),
        out_shape=jax.ShapeDtypeStruct((b, s, HG_WIDTH), BF16 if fuse else F32),
        scratch_shapes=[pltpu.VMEM((HG_DIM, HG_DIM), F32)],
        compiler_params=_cparams(("parallel", "parallel", "arbitrary")),
        name="hgrn_bwd" if reverse else "hgrn_fwd",
    )(*args)


def _merge_kernel(oa_ref, ob_ref, ga_ref, gb_ref, x_ref, g1_ref, n2_ref, sh2_ref, sc2_ref,
                  wpa_ref, wpb_ref, wo_ref, x1_ref, h2_ref, y_scr):
    tm, d = x1_ref.shape[1:]
    nc = _pick(d, (512, 256, 128))
    oa, ob = oa_ref[0], ob_ref[0]
    for n0 in range(0, d, nc):
        cs = slice(n0, n0 + nc)
        ya = jnp.dot(oa, wpa_ref[:, cs], preferred_element_type=F32)
        yb = jnp.dot(ob, wpb_ref[:, cs], preferred_element_type=F32)
        y_scr[:, cs] = (jax.nn.sigmoid(ga_ref[0, :, cs].astype(F32)) * ya
                        + jax.nn.sigmoid(gb_ref[0, :, cs].astype(F32)) * yb).astype(BF16)
    y = y_scr[...]
    ss = jnp.zeros((tm, 1), F32)
    for n0 in range(0, d, nc):
        cs = slice(n0, n0 + nc)
        x1 = x_ref[0, :, cs] + g1_ref[0, :, cs] * jnp.dot(y, wo_ref[:, cs], preferred_element_type=F32)
        x1_ref[0, :, cs] = x1
        ss = ss + jnp.sum(x1 * x1, axis=-1, keepdims=True)
    r = lax.rsqrt(ss * (1.0 / d) + EPS)
    h2_ref[0] = (x1_ref[0] * r * n2_ref[...] * (1.0 + sc2_ref[0]) + sh2_ref[0]).astype(BF16)


def _merge(o_a, o_b, mix, x, mod, n2, wpa, wpb, wo):
    b, s, d = x.shape
    tm = _pick(s, (512, 256, 128))
    const = lambda shape: pl.BlockSpec(shape, lambda bi, i: (0,) * len(shape), pipeline_mode=pl.Buffered(1))
    vec = lambda j: _mod_spec(d, 2, lambda bi: bi, j)
    return pl.pallas_call(
        _merge_kernel,
        grid=(b, s // tm),
        in_specs=[pl.BlockSpec((1, tm, NA_WIDTH), lambda bi, i: (bi, i, 0)),
                  pl.BlockSpec((1, tm, HG_WIDTH), lambda bi, i: (bi, i, 0)),
                  pl.BlockSpec((1, tm, d), lambda bi, i: (bi, i, 0)),
                  pl.BlockSpec((1, tm, d), lambda bi, i: (bi, i, 1)),
                  pl.BlockSpec((1, tm, d), lambda bi, i: (bi, i, 0)),
                  vec(2), const((1, d)), vec(3), vec(4),
                  const(wpa.shape), const(wpb.shape), const(wo.shape)],
        out_specs=[pl.BlockSpec((1, tm, d), lambda bi, i: (bi, i, 0)),
                   pl.BlockSpec((1, tm, d), lambda bi, i: (bi, i, 0))],
        out_shape=[jax.ShapeDtypeStruct((b, s, d), F32), jax.ShapeDtypeStruct((b, s, d), BF16)],
        scratch_shapes=[pltpu.VMEM((tm, d), BF16)],
        compiler_params=_cparams(("parallel", "parallel")),
        name="merge",
    )(o_a, o_b, mix, mix, x, mod, n2, mod, mod, wpa, wpb, wo)


def _ffn_kernel(nf, nx, h_ref, x1c_ref, g2_ref, fg_ref, wa_ref, wu_ref, wo_ref, o_ref, x1_scr):
    j = pl.program_id(2)
    tm, d = o_ref.shape[1:]
    rows = min(tm, MXU_ROWS_PER_DOT)
    nc = _pick(d, (512, 256, 128))
    xr = tm // nx

    @pl.when(j < nx)
    def _():
        x1_scr[pl.ds(pl.multiple_of(j * xr, xr), xr), :] = x1c_ref[0]

    def hidden_slice(first, last):
        for r0 in range(0, tm, rows):
            rs = slice(r0, r0 + rows)
            h = h_ref[0, rs]
            a = jnp.dot(h, wa_ref[...], preferred_element_type=F32)
            u = jnp.dot(h, wu_ref[...], preferred_element_type=F32)
            gate = (a * jax.nn.sigmoid(a) * u).astype(BF16)
            for n0 in range(0, d, nc):
                part = jnp.dot(gate, wo_ref[:, n0:n0 + nc], preferred_element_type=F32)
                if first:
                    o_ref[0, rs, n0:n0 + nc] = part
                else:
                    o_ref[0, rs, n0:n0 + nc] += part
            if last:
                x2 = x1_scr[rs, :] + g2_ref[0] * o_ref[0, rs]
                ms = jnp.mean(x2 * x2, axis=-1, keepdims=True)
                o_ref[0, rs] = x2 * lax.rsqrt(ms + EPS) * fg_ref[...]

    if nf == 1:
        hidden_slice(True, True)
    else:
        pl.when(j == 0)(functools.partial(hidden_slice, True, False))
        if nf > 2:
            pl.when((j > 0) & (j < nf - 1))(functools.partial(hidden_slice, False, False))
        pl.when(j == nf - 1)(functools.partial(hidden_slice, False, True))


def _ffn(h2, x1, mod, final_g, w_in, w_out):
    b, s, d = x1.shape
    fh = w_out.shape[0]
    tf = _pick(fh, (512, 256, 128))
    nf = fh // tf
    tm = _pick(s, (1024, 512, 256, 128))
    nx = max(n for n in (1, 2, 4, 8) if n <= nf and tm % (8 * n) == 0)
    return pl.pallas_call(
        functools.partial(_ffn_kernel, nf, nx),
        grid=(b, s // tm, nf),
        in_specs=[pl.BlockSpec((1, tm, d), lambda bi, i, j: (bi, i, 0)),
                  pl.BlockSpec((1, tm // nx, d), lambda bi, i, j: (bi, i * nx + jnp.minimum(j, nx - 1), 0)),
                  _mod_spec(d, 3, lambda bi: bi, 5),
                  pl.BlockSpec((1, d), lambda bi, i, j: (0, 0)),
                  pl.BlockSpec((d, tf), lambda bi, i, j: (0, j)),
                  pl.BlockSpec((d, tf), lambda bi, i, j: (0, nf + j)),
                  pl.BlockSpec((tf, d), lambda bi, i, j: (j, 0))],
        out_specs=pl.BlockSpec((1, tm, d), lambda bi, i, j: (bi, i, 0)),
        out_shape=jax.ShapeDtypeStruct((b, s, d), F32),
        scratch_shapes=[pltpu.VMEM((tm, d), F32)],
        compiler_params=_cparams(("parallel", "parallel", "arbitrary")),
        name="ffn",
    )(h2, x1, mod, final_g.reshape(1, d), w_in, w_in, w_out)


def kernel(x, c, ctx, c_ctx, w_ada, b_ada, norm1_g, w_in, na_rpb, hg_lb_logits, hg_norm_g,
           w_pa, w_pb, w_out, norm2_g, w_ffn_in, w_ffn_out, final_g):
    b, s, d = x.shape
    assert w_ada.shape[0] == 1, "single layer"

    lb_table = jnp.cumsum(jax.nn.softmax(hg_lb_logits.astype(F32), axis=0), axis=0)
    lb_f, lb_b = lb_table[0, 0], lb_table[0, 1]

    c8 = jnp.zeros((8, d), F32).at[:b].set(c).at[b].set(c_ctx)
    mod = _ada(c8, w_ada[0], b_ada[0]).reshape(8 * N_MOD, 1, d)

    assert NA_WIDTH == HG_WIDTH == INPROJ_TN and (2 * d) % INPROJ_TN == 0
    q_, k_, v_, hq_, ff_, fb_, hi_, hog_ = range(8)
    gates = list(range(8, 8 + 2 * d // INPROJ_TN))
    w_bf = w_in[0].astype(BF16)
    n1 = norm1_g[0].reshape(1, d)

    qkv, mix, fg = _norm_inproj(x, n1, mod, lambda bi: bi, w_bf,
                                [(BF16, [q_, k_, v_]), (BF16, gates + [hq_, hi_, hog_]), (F32, [ff_, fb_])])
    kv_c, hi_c, fg_c = _norm_inproj(ctx, n1, mod, lambda bi: b, w_bf,
                                    [(BF16, [k_, v_]), (BF16, [hi_]), (F32, [ff_, fb_])])

    o_na = _na(qkv, kv_c, _na_bias_table(na_rpb[0]))

    cf, cb = _hg_constants(False), _hg_constants(True)
    s_f = _hgrn_state(hi_c, fg_c, lb_f, cf[0], False)
    s_b = _hgrn_state(hi_c, fg_c, lb_b, cb[0], True)
    o_f = _hgrn(mix, fg, lb_f, s_f, cf, False, d)
    o_hg = _hgrn(mix, fg, lb_b, s_b, cb, True, d, o_fwd=o_f, gn=hg_norm_g[0])

    x1, h2 = _merge(o_na, o_hg, mix, x, mod, norm2_g[0].reshape(1, d),
                    w_pa[0].astype(BF16), w_pb[0].astype(BF16), w_out[0].astype(BF16))
    return _ffn(h2, x1, mod, final_g, w_ffn_in[0].astype(BF16), w_ffn_out[0].astype(BF16))
```

```python
import functools

import jax
import jax.numpy as jnp
import numpy as np
from jax import lax
from jax.experimental import pallas as pl
from jax.experimental.pallas import tpu as pltpu

F32 = jnp.float32
BF16 = jnp.bfloat16

GRID_W = 64
WIN_H = 8
WIN_W = 16
NA_HEADS = 16
NA_HEAD_DIM = 64
NA_WIDTH = NA_HEADS * NA_HEAD_DIM
HG_HEADS = 8
HG_DIM = 128
HG_WIDTH = HG_HEADS * HG_DIM
N_MOD = 6
EPS = 1e-6
NEG_BIG = -1e30

V7X_VMEM_LIMIT_BYTES = 60000 * 1024
MXU_ROWS_PER_DOT = 512
NA_SLAB_HEADS = 4
NA_ROWS_PER_STEP = 2
HG_CHUNK = 64
HG_LEVELS = 6
HG_VPU_LEVELS = 1
LOG2_E = 1.4426950408889634


def _cparams(sem):
    return pltpu.CompilerParams(dimension_semantics=sem, vmem_limit_bytes=V7X_VMEM_LIMIT_BYTES)


def _pick(n, prefs):
    for p in prefs:
        if n % p == 0:
            return p
    return n


def _ada_kernel(c_ref, w_ref, b_ref, o_ref):
    c = c_ref[...]
    s = c * jax.nn.sigmoid(c)
    o_ref[...] = jnp.dot(s, w_ref[...], precision=lax.Precision.HIGHEST,
                         preferred_element_type=F32) + b_ref[...]


def _ada(c8, w, b):
    d, n = w.shape
    tn = _pick(n, (1024, 512, 256, 128))
    return pl.pallas_call(
        _ada_kernel,
        grid=(n // tn,),
        in_specs=[pl.BlockSpec((8, d), lambda j: (0, 0)),
                  pl.BlockSpec((d, tn), lambda j: (0, j)),
                  pl.BlockSpec((1, tn), lambda j: (0, j))],
        out_specs=pl.BlockSpec((8, tn), lambda j: (0, j)),
        out_shape=jax.ShapeDtypeStruct((8, n), F32),
        compiler_params=_cparams(("arbitrary",)),
        name="ada",
    )(c8, w, b.reshape(1, n))


INPROJ_TN = 1024
NORM_CHUNKS = (8, 4, 2, 1)


def _norm_inproj_kernel(starts, nx, x_ref, g_ref, sh_ref, sc_ref, w_ref, *refs):
    o_refs, h_bufs = refs[:-2], refs[-2:]
    i, j = pl.program_id(1), pl.program_id(2)
    tm = h_bufs[0].shape[0]
    xr = tm // nx
    rows = min(tm, MXU_ROWS_PER_DOT)

    def normalise_chunk(h_fill):
        x = x_ref[0]
        ms = jnp.mean(x * x, axis=-1, keepdims=True)
        y = x * lax.rsqrt(ms + EPS) * g_ref[...]
        r0 = pl.multiple_of(jnp.minimum(j, nx - 1) * xr, xr)
        h_fill[pl.ds(r0, xr), :] = (y * (1.0 + sc_ref[0]) + sh_ref[0]).astype(BF16)

    def project(o_ref, h_fill, h_use):
        normalise_chunk(h_fill)
        for r0 in range(0, tm, rows):
            o_ref[0, r0:r0 + rows] = jnp.dot(h_use[r0:r0 + rows], w_ref[...],
                                             preferred_element_type=F32).astype(o_ref.dtype)

    pl.when(i == 0)(functools.partial(normalise_chunk, h_bufs[0]))
    for parity in (0, 1):
        for k, o_ref in enumerate(o_refs):
            pl.when((i > 0) & (i % 2 == parity) & (j >= starts[k]) & (j < starts[k + 1]))(
                functools.partial(project, o_ref, h_bufs[parity], h_bufs[1 - parity]))


def _mod_spec(d, grid_rank, row, j):
    if grid_rank == 2:
        return pl.BlockSpec((1, 1, d), lambda bi, i: (row(bi) * N_MOD + j, 0, 0))
    return pl.BlockSpec((1, 1, d), lambda bi, i, k: (row(bi) * N_MOD + j, 0, 0))


def _norm_inproj(x, g, mod, row, w, outputs):
    b, t, d = x.shape
    tn = INPROJ_TN
    tm = _pick(t, (1024, 512, 256))
    ni = t // tm
    counts = [len(blocks) for _, blocks in outputs]
    starts = [int(v) for v in np.cumsum([0] + counts)]
    nj = starts[-1]
    table = [blk for _, blocks in outputs for blk in blocks]
    nx = next(n for n in NORM_CHUNKS if n <= nj and tm % (16 * n) == 0)
    xr = tm // nx

    def w_block(i, j):
        r = jnp.int32(table[-1])
        for idx in range(len(table) - 2, -1, -1):
            r = jnp.where(j == idx, table[idx], r)
        return jnp.where(i == 0, table[0], r)

    def x_block(i, j):
        return jnp.where(i >= ni, ni * nx - 1, i * nx + jnp.minimum(j, nx - 1))

    def o_spec(k):
        return pl.BlockSpec((1, tm, tn), lambda bi, i, j: (
            bi, jnp.maximum(i - 1, 0), jnp.where(i == 0, 0, jnp.clip(j - starts[k], 0, counts[k] - 1))))

    return pl.pallas_call(
        functools.partial(_norm_inproj_kernel, starts, nx),
        grid=(b, ni + 1, nj),
        in_specs=[pl.BlockSpec((1, xr, d), lambda bi, i, j: (bi, x_block(i, j), 0)),
                  pl.BlockSpec((1, d), lambda bi, i, j: (0, 0)),
                  _mod_spec(d, 3, row, 0), _mod_spec(d, 3, row, 1),
                  pl.BlockSpec((d, tn), lambda bi, i, j: (0, w_block(i, j)))],
        out_specs=[o_spec(k) for k in range(len(outputs))],
        out_shape=[jax.ShapeDtypeStruct((b, t, n * tn), dt) for (dt, _), n in zip(outputs, counts)],
        scratch_shapes=[pltpu.VMEM((tm, d), BF16), pltpu.VMEM((tm, d), BF16)],
        compiler_params=_cparams(("parallel", "arbitrary", "arbitrary")),
        name="norm_inproj",
    )(x, g, mod, mod, w)


def _na_kernel(nb, nr, q_ref, *refs):
    kb_refs, vb_refs, (kc_ref, vc_ref), bias_refs, o_ref = (refs[:nr], refs[nr:2 * nr], refs[2 * nr:2 * nr + 2],
                                                            refs[2 * nr + 2:3 * nr + 2], refs[3 * nr + 2])
    dn = (((1,), (1,)), ((), ()))
    nh, w = NA_SLAB_HEADS, NA_SLAB_HEADS * NA_HEAD_DIM
    hq = nh * GRID_W
    lane_head = lax.broadcasted_iota(jnp.int32, (GRID_W, w), 1) // NA_HEAD_DIM
    own = [lane_head == a for a in range(nh)]
    slabs = range(NA_HEADS // nh)
    work = [(j, g) for j in range(nr) for g in slabs]
    cols = {g: slice(g * w, (g + 1) * w) for g in slabs}
    q = {}
    for j, g in work:
        qg = q_ref[0, j * GRID_W:(j + 1) * GRID_W, cols[g]] * (NA_HEAD_DIM ** -0.5)
        q[j, g] = jnp.concatenate([jnp.where(own[a], qg, jnp.zeros_like(qg)) for a in range(nh)], axis=0)
    s_b = {(j, g): lax.dot_general(q[j, g], kb_refs[j][0, :, cols[g]], dn, preferred_element_type=F32)
           + bias_refs[j][0, g * nh:(g + 1) * nh].reshape(hq, nb) for j, g in work}
    s_c = {}
    for g in slabs:
        both = lax.dot_general(jnp.concatenate([q[j, g] for j in range(nr)], axis=0), kc_ref[0, :, cols[g]], dn,
                               preferred_element_type=F32)
        for j in range(nr):
            s_c[j, g] = both[j * hq:(j + 1) * hq]
    m = {k: jnp.maximum(jnp.max(s_b[k], axis=-1, keepdims=True), jnp.max(s_c[k], axis=-1, keepdims=True))
         for k in work}
    p_b = {k: jnp.exp(s_b[k] - m[k]) for k in work}
    p_c = {k: jnp.exp(s_c[k] - m[k]) for k in work}
    l = {k: jnp.sum(p_b[k], axis=-1, keepdims=True) + jnp.sum(p_c[k], axis=-1, keepdims=True) for k in work}
    o_c = {}
    for g in slabs:
        both = jnp.dot(jnp.concatenate([p_c[j, g] for j in range(nr)], axis=0).astype(BF16), vc_ref[0, :, cols[g]],
                       preferred_element_type=F32)
        for j in range(nr):
            o_c[j, g] = both[j * hq:(j + 1) * hq]
    o = {(j, g): (jnp.dot(p_b[j, g].astype(BF16), vb_refs[j][0, :, cols[g]], preferred_element_type=F32)
                  + o_c[j, g]) / l[j, g] for j, g in work}
    for j, g in work:
        out = o[j, g][0:GRID_W]
        for a in range(1, nh):
            out = jnp.where(own[a], o[j, g][a * GRID_W:(a + 1) * GRID_W], out)
        o_ref[0, j * GRID_W:(j + 1) * GRID_W, cols[g]] = out.astype(BF16)


def _na_bias_table(rpb):
    col = np.arange(GRID_W)
    col_start = np.clip(col - WIN_W // 2, 0, GRID_W - WIN_W)
    in_win = (col[None, :] >= col_start[:, None]) & (col[None, :] < col_start[:, None] + WIN_W)
    dc_idx = np.clip(col[None, :] - col[:, None], 1 - WIN_W, WIN_W - 1) + WIN_W - 1
    pick_col = (dc_idx[None] == np.arange(2 * WIN_W - 1)[:, None, None]).astype(np.float32)
    off, row = np.arange(WIN_H)[:, None, None], np.arange(WIN_H)[None, :, None]
    pick_row = (off + row == np.arange(2 * WIN_H - 1)[None, None, :]).astype(np.float32)
    by_row = jnp.einsum('oid,hdc->ohic', jnp.asarray(pick_row), rpb.astype(F32), precision=lax.Precision.HIGHEST)
    tab = jnp.einsum('ohic,cqk->ohqik', by_row, jnp.asarray(pick_col), precision=lax.Precision.HIGHEST)
    tab = jnp.where(jnp.asarray(in_win)[None, None, :, None, :], tab, NEG_BIG)
    return tab.reshape(WIN_H, NA_HEADS, GRID_W, WIN_H * GRID_W)


def _na(qkv, kv_c, bias_tab):
    b, s, _ = qkv.shape
    lc = kv_c.shape[1]
    rows = s // GRID_W
    nr = _pick(rows, (NA_ROWS_PER_STEP, 1))
    assert rows >= WIN_H
    nb = WIN_H * GRID_W

    def rs_of(r):
        return jnp.clip(r - WIN_H // 2, 0, rows - WIN_H)

    band = (pl.Element(1), pl.Element(nb), pl.Element(NA_WIDTH))
    k_band = [pl.BlockSpec(band, lambda bi, rr, j=j: (bi, rs_of(rr * nr + j) * GRID_W, NA_WIDTH))
              for j in range(nr)]
    v_band = [pl.BlockSpec(band, lambda bi, rr, j=j: (bi, rs_of(rr * nr + j) * GRID_W, 2 * NA_WIDTH))
              for j in range(nr)]
    bias = [pl.BlockSpec((1, NA_HEADS, GRID_W, nb),
                         lambda bi, rr, j=j: (rs_of(rr * nr + j) - (rr * nr + j) + WIN_H - 1, 0, 0, 0))
            for j in range(nr)]
    return pl.pallas_call(
        functools.partial(_na_kernel, nb, nr),
        grid=(b, rows // nr),
        in_specs=[pl.BlockSpec((1, nr * GRID_W, NA_WIDTH), lambda bi, rr: (bi, rr, 0)), *k_band, *v_band,
                  pl.BlockSpec((1, lc, NA_WIDTH), lambda bi, rr: (bi, 0, 0)),
                  pl.BlockSpec((1, lc, NA_WIDTH), lambda bi, rr: (bi, 0, 1)), *bias],
        out_specs=pl.BlockSpec((1, nr * GRID_W, NA_WIDTH), lambda bi, rr: (bi, rr, 0)),
        out_shape=jax.ShapeDtypeStruct((b, s, NA_WIDTH), BF16),
        compiler_params=_cparams(("parallel", "arbitrary")),
        name="na",
    )(qkv, *([qkv] * (2 * nr)), kv_c, kv_c, *([bias_tab] * nr))


def _hg_constants(reverse):
    c = HG_CHUNK
    pos = np.arange(c)
    pmat = np.tril(np.ones((c, c), np.float32))
    qsel = np.zeros((HG_LEVELS, c, HG_DIM), np.float32)
    amask = np.zeros((HG_LEVELS - 1, c, c), np.float32)
    for l in range(HG_LEVELS):
        m = 1 << l
        blk = pos // (2 * m)
        is_q = (pos % (2 * m)) >= m
        qsel[l] = is_q[:, None]
        if l > 0:
            amask[l - 1] = (blk[:, None] == blk[None, :]) & is_q[:, None] & ~is_q[None, :]
    if reverse:
        pmat = pmat[::-1, ::-1]
        qsel = qsel[:, ::-1]
        amask = amask[:, ::-1, ::-1]
    return (jnp.asarray(np.concatenate([pmat, pmat, pmat], axis=1), BF16),
            jnp.asarray(qsel, F32), jnp.asarray(amask, F32))


def _boundary_rows(p, l, reverse):
    c = HG_CHUNK
    m = 1 << l
    off = m if reverse else m - 1
    if 2 * m >= 16:
        return jnp.concatenate([jnp.broadcast_to(p[b0 + off:b0 + off + 1], (2 * m, HG_DIM))
                                for b0 in range(0, c, 2 * m)], axis=0)
    p3 = p.reshape(c // 8, 8, HG_DIM)
    pick = lambda s: jnp.broadcast_to(p3[:, s:s + 1], p3.shape)
    if 2 * m == 8:
        return pick(off).reshape(c, HG_DIM)
    assert 2 * m == 4
    upper = lax.broadcasted_iota(jnp.int32, p3.shape, 1) >= 4
    return jnp.where(upper, pick(4 + off), pick(off)).reshape(c, HG_DIM)


def _chunk_prefixes(pmat, pieces, order):
    c = HG_CHUNK
    rows = lambda a, ci: a[ci * c:(ci + 1) * c]
    pre = {}
    for ca, cb in zip(order[0::2], order[1::2]):
        rhs = jnp.concatenate([jnp.concatenate([rows(a, ca), rows(a, cb)], axis=1) for a in pieces], axis=0)
        both = jnp.dot(pmat, rhs, preferred_element_type=F32)
        pre[ca], pre[cb] = both[:, :HG_DIM], both[:, HG_DIM:]
    if len(order) % 2:
        ci = order[-1]
        pre[ci] = jnp.dot(pmat, jnp.concatenate([rows(a, ci) for a in pieces], axis=0),
                          preferred_element_type=F32)
    return pre


def _split3(a):
    hi = a.astype(BF16)
    r1 = a - hi.astype(F32)
    mid = r1.astype(BF16)
    lo = (r1 - mid.astype(F32)).astype(BF16)
    return hi, mid, lo


def _hg_block(q, z, v, lb, st, pmat, qsel_ref, amask_ref, order, reverse):
    c = HG_CHUNK
    dn_nt = (((1,), (1,)), ((), ()))
    f = lb + (1.0 - lb) * jax.nn.sigmoid(z)
    kk = 1.0 - f
    qf = q.astype(F32)
    rows = lambda a, ci: a[ci * c:(ci + 1) * c]
    pre = _chunk_prefixes(pmat, _split3(jnp.log(f)), order)
    qsel = [qsel_ref[l] != 0.0 for l in range(HG_LEVELS)]
    amask = [amask_ref[l] != 0.0 for l in range(HG_LEVELS - 1)]
    a = {}
    for l in range(HG_VPU_LEVELS, HG_LEVELS):
        sign = jnp.where(qsel[l], LOG2_E, -LOG2_E)
        w = {ci: (jnp.exp2((pre[ci] - _boundary_rows(pre[ci], l, reverse)) * sign)
                  * jnp.where(qsel[l], rows(qf, ci), rows(kk, ci))).astype(BF16) for ci in order}
        p = {ci: lax.dot_general(w[ci], w[ci], dn_nt, preferred_element_type=F32) for ci in order}
        a = {ci: jnp.where(amask[l - 1], p[ci], a[ci] if l > HG_VPU_LEVELS else 0.0) for ci in order}
    last = 0 if reverse else c - 1
    e_in = {ci: jnp.exp(pre[ci]) for ci in order}
    e_out = {ci: jnp.exp(pre[ci][last:last + 1] - pre[ci]) for ci in order}
    upd = {ci: lax.dot_general(rows(v, ci), (rows(kk, ci) * e_out[ci]).astype(BF16), (((0,), (0,)), ((), ())),
                               preferred_element_type=F32) for ci in order}
    o_intra = {ci: jnp.dot(a[ci].astype(BF16), rows(v, ci), preferred_element_type=F32) for ci in order}
    vf = v.astype(F32)
    small = 1 << HG_VPU_LEVELS
    row = lax.broadcasted_iota(jnp.int32, (c, 1), 0)
    in_block = ((c - 1 - row) if reverse else row) % small
    back = lambda x, dist: pltpu.roll(x, (c - dist) if reverse else dist, 0)
    for ci in order:
        qc, kc, vc, fc = rows(qf, ci), rows(kk, ci), rows(vf, ci), rows(f, ci)
        acc = o_intra[ci] + jnp.sum(qc * kc, axis=-1, keepdims=True) * vc
        decay = fc
        for dist in range(1, small):
            if dist > 1:
                decay = decay * back(fc, dist - 1)
            wgt = jnp.sum(qc * decay * back(kc, dist), axis=-1, keepdims=True)
            acc = acc + jnp.where(in_block >= dist, wgt, 0.0) * back(vc, dist)
        o_intra[ci] = acc
    qd = {ci: (rows(qf, ci) * e_in[ci]).astype(BF16) for ci in order}
    out = {}
    for ci in order:
        out[ci] = o_intra[ci] + lax.dot_general(qd[ci], st.astype(BF16), dn_nt, preferred_element_type=F32)
        st = st * e_in[ci][last:last + 1] + upd[ci]
    return out, st


def _hg_kernel(reverse, fuse_readout, nchunk, *refs):
    if fuse_readout:
        (q_ref, v_ref, z_ref, lb_ref, s0_ref, dmat_ref, qsel_ref, amask_ref, of_ref, og_ref, gn_ref,
         o_ref, st_scr) = refs
    else:
        (q_ref, v_ref, z_ref, lb_ref, s0_ref, dmat_ref, qsel_ref, amask_ref,
         o_ref, st_scr) = refs
    t = pl.program_id(2)

    @pl.when(t == 0)
    def _():
        st_scr[...] = s0_ref[0, 0]

    order = list(range(nchunk - 1, -1, -1) if reverse else range(nchunk))
    out, st = _hg_block(q_ref[0], z_ref[0], v_ref[0], lb_ref[0], st_scr[...], dmat_ref[...],
                        qsel_ref, amask_ref, order, reverse)
    o = jnp.concatenate([out[ci] for ci in range(nchunk)], axis=0)
    if fuse_readout:
        o = o + of_ref[0]
        on = o * lax.rsqrt(jnp.mean(o * o, axis=-1, keepdims=True) + EPS) * gn_ref[...]
        g = og_ref[0].astype(F32)
        o_ref[0] = (on * (g * jax.nn.sigmoid(g))).astype(o_ref.dtype)
    else:
        o_ref[0] = o
    st_scr[...] = st


def _hg_state_kernel(reverse, nchunk, v_ref, z_ref, lb_ref, pmat_ref, sout_ref):
    c = HG_CHUNK
    v = v_ref[0]
    f = lb_ref[...] + (1.0 - lb_ref[...]) * jax.nn.sigmoid(z_ref[0])
    kk = 1.0 - f
    pieces = _split3(jnp.log(f))
    order = list(range(nchunk - 1, -1, -1) if reverse else range(nchunk))
    last = 0 if reverse else c - 1
    tile = lambda a, h, ci: a[ci * c:(ci + 1) * c, h * HG_DIM:(h + 1) * HG_DIM]
    pre, upd = {}, {}
    for h in range(HG_HEADS):
        head = [a[:, h * HG_DIM:(h + 1) * HG_DIM] for a in pieces]
        for ci, p in _chunk_prefixes(pmat_ref[...], head, order).items():
            pre[h, ci] = p
    for k, p in pre.items():
        e_out = jnp.exp(p[last:last + 1] - p)
        upd[k] = lax.dot_general(tile(v, *k), (tile(kk, *k) * e_out).astype(BF16), (((0,), (0,)), ((), ())),
                                 preferred_element_type=F32)
    for h in range(HG_HEADS):
        st = jnp.zeros((HG_DIM, HG_DIM), F32)
        for ci in order:
            st = st * jnp.exp(pre[h, ci][last:last + 1]) + upd[h, ci]
        sout_ref[0, h] = st


def _hgrn_state(val, fg, lb, dmat, reverse):
    b, l, _ = val.shape
    assert l % HG_CHUNK == 0
    fblk = 1 if reverse else 0
    return pl.pallas_call(
        functools.partial(_hg_state_kernel, reverse, l // HG_CHUNK),
        grid=(b,),
        in_specs=[pl.BlockSpec((1, l, HG_WIDTH), lambda bi: (bi, 0, 0)),
                  pl.BlockSpec((1, l, HG_WIDTH), lambda bi: (bi, 0, fblk)),
                  pl.BlockSpec((1, HG_WIDTH), lambda bi: (0, 0)),
                  pl.BlockSpec(dmat.shape, lambda bi: (0, 0))],
        out_specs=pl.BlockSpec((1, HG_HEADS, HG_DIM, HG_DIM), lambda bi: (bi, 0, 0, 0)),
        out_shape=jax.ShapeDtypeStruct((b, HG_HEADS, HG_DIM, HG_DIM), F32),
        compiler_params=_cparams(("parallel",)),
        name="hgrn_state_bwd" if reverse else "hgrn_state_fwd",
    )(val, fg, lb.reshape(1, HG_WIDTH), dmat)


def _hgrn(mix, fg, lb, s0, consts, reverse, d, o_fwd=None, gn=None):
    b, s, _ = mix.shape
    tt = _pick(s, (2048, 1024, 512, 256, 128, 64))
    nt = s // tt
    base = 2 * d // HG_DIM
    fuse = o_fwd is not None
    dmat, qsel, amask = consts
    fcol = HG_HEADS if reverse else 0

    def tmap(t):
        return nt - 1 - t if reverse else t

    blk = (1, tt, HG_DIM)
    in_specs = [pl.BlockSpec(blk, lambda bi, h, t: (bi, tmap(t), base + h)),
                pl.BlockSpec(blk, lambda bi, h, t: (bi, tmap(t), base + HG_HEADS + h)),
                pl.BlockSpec(blk, lambda bi, h, t: (bi, tmap(t), fcol + h)),
                pl.BlockSpec((1, 1, HG_DIM), lambda bi, h, t: (h, 0, 0)),
                pl.BlockSpec((1, 1, HG_DIM, HG_DIM), lambda bi, h, t: (bi, h, 0, 0)),
                pl.BlockSpec(dmat.shape, lambda bi, h, t: (0, 0)),
                pl.BlockSpec(qsel.shape, lambda bi, h, t: (0, 0, 0)),
                pl.BlockSpec(amask.shape, lambda bi, h, t: (0, 0, 0))]
    args = [mix, mix, fg, lb.reshape(HG_HEADS, 1, HG_DIM), s0, dmat, qsel, amask]
    if fuse:
        in_specs += [pl.BlockSpec(blk, lambda bi, h, t: (bi, tmap(t), h)),
                     pl.BlockSpec(blk, lambda bi, h, t: (bi, tmap(t), base + 2 * HG_HEADS + h)),
                     pl.BlockSpec((1, HG_DIM), lambda bi, h, t: (0, 0))]
        args += [o_fwd, mix, gn.reshape(1, HG_DIM)]
    return pl.pallas_call(
        functools.partial(_hg_kernel, reverse, fuse, tt // HG_CHUNK),
        grid=(b, HG_HEADS, nt),
        in_specs=in_specs,
        out_specs=pl.BlockSpec(blk, lambda bi, h, t: (bi, tmap(t), h)),
        out_shape=jax.ShapeDtypeStruct((b, s, HG_WIDTH), BF16 if fuse else F32),
        scratch_shapes=[pltpu.VMEM((HG_DIM, HG_DIM), F32)],
        compiler_params=_cparams(("parallel", "parallel", "arbitrary")),
        name="hgrn_bwd" if reverse else "hgrn_fwd",
    )(*args)


def _merge_kernel(oa_ref, ob_ref, ga_ref, gb_ref, x_ref, g1_ref, n2_ref, sh2_ref, sc2_ref,
                  wpa_ref, wpb_ref, wo_ref, x1_ref, h2_ref, y_scr):
    tm, d = x1_ref.shape[1:]
    nc = _pick(d, (512, 256, 128))
    oa, ob = oa_ref[0], ob_ref[0]
    for n0 in range(0, d, nc):
        cs = slice(n0, n0 + nc)
        ya = jnp.dot(oa, wpa_ref[:, cs], preferred_element_type=F32)
        yb = jnp.dot(ob, wpb_ref[:, cs], preferred_element_type=F32)
        y_scr[:, cs] = (jax.nn.sigmoid(ga_ref[0, :, cs].astype(F32)) * ya
                        + jax.nn.sigmoid(gb_ref[0, :, cs].astype(F32)) * yb).astype(BF16)
    y = y_scr[...]
    ss = jnp.zeros((tm, 1), F32)
    for n0 in range(0, d, nc):
        cs = slice(n0, n0 + nc)
        x1 = x_ref[0, :, cs] + g1_ref[0, :, cs] * jnp.dot(y, wo_ref[:, cs], preferred_element_type=F32)
        x1_ref[0, :, cs] = x1
        ss = ss + jnp.sum(x1 * x1, axis=-1, keepdims=True)
    r = lax.rsqrt(ss * (1.0 / d) + EPS)
    h2_ref[0] = (x1_ref[0] * r * n2_ref[...] * (1.0 + sc2_ref[0]) + sh2_ref[0]).astype(BF16)


def _merge(o_a, o_b, mix, x, mod, n2, wpa, wpb, wo):
    b, s, d = x.shape
    tm = _pick(s, (512, 256, 128))
    const = lambda shape: pl.BlockSpec(shape, lambda bi, i: (0,) * len(shape), pipeline_mode=pl.Buffered(1))
    vec = lambda j: _mod_spec(d, 2, lambda bi: bi, j)
    return pl.pallas_call(
        _merge_kernel,
        grid=(b, s // tm),
        in_specs=[pl.BlockSpec((1, tm, NA_WIDTH), lambda bi, i: (bi, i, 0)),
                  pl.BlockSpec((1, tm, HG_WIDTH), lambda bi, i: (bi, i, 0)),
                  pl.BlockSpec((1, tm, d), lambda bi, i: (bi, i, 0)),
                  pl.BlockSpec((1, tm, d), lambda bi, i: (bi, i, 1)),
                  pl.BlockSpec((1, tm, d), lambda bi, i: (bi, i, 0)),
                  vec(2), const((1, d)), vec(3), vec(4),
                  const(wpa.shape), const(wpb.shape), const(wo.shape)],
        out_specs=[pl.BlockSpec((1, tm, d), lambda bi, i: (bi, i, 0)),
                   pl.BlockSpec((1, tm, d), lambda bi, i: (bi, i, 0))],
        out_shape=[jax.ShapeDtypeStruct((b, s, d), F32), jax.ShapeDtypeStruct((b, s, d), BF16)],
        scratch_shapes=[pltpu.VMEM((tm, d), BF16)],
        compiler_params=_cparams(("parallel", "parallel")),
        name="merge",
    )(o_a, o_b, mix, mix, x, mod, n2, mod, mod, wpa, wpb, wo)


def _ffn_kernel(nf, nx, h_ref, x1c_ref, g2_ref, fg_ref, wa_ref, wu_ref, wo_ref, o_ref, x1_scr):
    j = pl.program_id(2)
    tm, d = o_ref.shape[1:]
    rows = min(tm, MXU_ROWS_PER_DOT)
    nc = _pick(d, (512, 256, 128))
    xr = tm // nx

    @pl.when(j < nx)
    def _():
        x1_scr[pl.ds(pl.multiple_of(j * xr, xr), xr), :] = x1c_ref[0]

    def hidden_slice(first, last):
        for r0 in range(0, tm, rows):
            rs = slice(r0, r0 + rows)
            h = h_ref[0, rs]
            a = jnp.dot(h, wa_ref[...], preferred_element_type=F32)
            u = jnp.dot(h, wu_ref[...], preferred_element_type=F32)
            gate = (a * jax.nn.sigmoid(a) * u).astype(BF16)
            for n0 in range(0, d, nc):
                part = jnp.dot(gate, wo_ref[:, n0:n0 + nc], preferred_element_type=F32)
                if first:
                    o_ref[0, rs, n0:n0 + nc] = part
                else:
                    o_ref[0, rs, n0:n0 + nc] += part
            if last:
                x2 = x1_scr[rs, :] + g2_ref[0] * o_ref[0, rs]
                ms = jnp.mean(x2 * x2, axis=-1, keepdims=True)
                o_ref[0, rs] = x2 * lax.rsqrt(ms + EPS) * fg_ref[...]

    if nf == 1:
        hidden_slice(True, True)
    else:
        pl.when(j == 0)(functools.partial(hidden_slice, True, False))
        if nf > 2:
            pl.when((j > 0) & (j < nf - 1))(functools.partial(hidden_slice, False, False))
        pl.when(j == nf - 1)(functools.partial(hidden_slice, False, True))


def _ffn(h2, x1, mod, final_g, w_in, w_out):
    b, s, d = x1.shape
    fh = w_out.shape[0]
    tf = _pick(fh, (512, 256, 128))
    nf = fh // tf
    tm = _pick(s, (1024, 512, 256, 128))
    nx = max(n for n in (1, 2, 4, 8) if n <= nf and tm % (8 * n) == 0)
    return pl.pallas_call(
        functools.partial(_ffn_kernel, nf, nx),
        grid=(b, s // tm, nf),
        in_specs=[pl.BlockSpec((1, tm, d), lambda bi, i, j: (bi, i, 0)),
                  pl.BlockSpec((1, tm // nx, d), lambda bi, i, j: (bi, i * nx + jnp.minimum(j, nx - 1), 0)),
                  _mod_spec(d, 3, lambda bi: bi, 5),
                  pl.BlockSpec((1, d), lambda bi, i, j: (0, 0)),
                  pl.BlockSpec((d, tf), lambda bi, i, j: (0, j)),
                  pl.BlockSpec((d, tf), lambda bi, i, j: (0, nf + j)),
                  pl.BlockSpec((tf, d), lambda bi, i, j: (j, 0))],
        out_specs=pl.BlockSpec((1, tm, d), lambda bi, i, j: (bi, i, 0)),
        out_shape=jax.ShapeDtypeStruct((b, s, d), F32),
        scratch_shapes=[pltpu.VMEM((tm, d), F32)],
        compiler_params=_cparams(("parallel", "parallel", "arbitrary")),
        name="ffn",
    )(h2, x1, mod, final_g.reshape(1, d), w_in, w_in, w_out)


def kernel(x, c, ctx, c_ctx, w_ada, b_ada, norm1_g, w_in, na_rpb, hg_lb_logits, hg_norm_g,
           w_pa, w_pb, w_out, norm2_g, w_ffn_in, w_ffn_out, final_g):
    b, s, d = x.shape
    assert w_ada.shape[0] == 1, "single layer"

    lb_table = jnp.cumsum(jax.nn.softmax(hg_lb_logits.astype(F32), axis=0), axis=0)
    lb_f, lb_b = lb_table[0, 0], lb_table[0, 1]

    c8 = jnp.zeros((8, d), F32).at[:b].set(c).at[b].set(c_ctx)
    mod = _ada(c8, w_ada[0], b_ada[0]).reshape(8 * N_MOD, 1, d)

    assert NA_WIDTH == HG_WIDTH == INPROJ_TN and (2 * d) % INPROJ_TN == 0
    q_, k_, v_, hq_, ff_, fb_, hi_, hog_ = range(8)
    gates = list(range(8, 8 + 2 * d // INPROJ_TN))
    w_bf = w_in[0].astype(BF16)
    n1 = norm1_g[0].reshape(1, d)

    qkv, mix, fg = _norm_inproj(x, n1, mod, lambda bi: bi, w_bf,
                                [(BF16, [q_, k_, v_]), (BF16, gates + [hq_, hi_, hog_]), (F32, [ff_, fb_])])
    kv_c, hi_c, fg_c = _norm_inproj(ctx, n1, mod, lambda bi: b, w_bf,
                                    [(BF16, [k_, v_]), (BF16, [hi_]), (F32, [ff_, fb_])])

    o_na = _na(qkv, kv_c, _na_bias_table(na_rpb[0]))

    cf, cb = _hg_constants(False), _hg_constants(True)
    s_f = _hgrn_state(hi_c, fg_c, lb_f, cf[0], False)
    s_b = _hgrn_state(hi_c, fg_c, lb_b, cb[0], True)
    o_f = _hgrn(mix, fg, lb_f, s_f, cf, False, d)
    o_hg = _hgrn(mix, fg, lb_b, s_b, cb, True, d, o_fwd=o_f, gn=hg_norm_g[0])

    x1, h2 = _merge(o_na, o_hg, mix, x, mod, norm2_g[0].reshape(1, d),
                    w_pa[0].astype(BF16), w_pb[0].astype(BF16), w_out[0].astype(BF16))
    return _ffn(h2, x1, mod, final_g, w_ffn_in[0].astype(BF16), w_ffn_out[0].astype(BF16))
```

```python
import functools

import jax
import jax.numpy as jnp
import numpy as np
from jax import lax
from jax.experimental import pallas as pl
from jax.experimental.pallas import tpu as pltpu

F32 = jnp.float32
BF16 = jnp.bfloat16

GRID_W = 64
WIN_H = 8
WIN_W = 16
NA_HEADS = 16
NA_HEAD_DIM = 64
NA_WIDTH = NA_HEADS * NA_HEAD_DIM
HG_HEADS = 8
HG_DIM = 128
HG_WIDTH = HG_HEADS * HG_DIM
N_MOD = 6
EPS = 1e-6
NEG_BIG = -1e30

V7X_VMEM_LIMIT_BYTES = 60000 * 1024
MXU_ROWS_PER_DOT = 512
NA_SLAB_HEADS = 4
NA_ROWS_PER_STEP = 2
HG_CHUNK = 64
HG_LEVELS = 6
HG_VPU_LEVELS = 1
LOG2_E = 1.4426950408889634


def _cparams(sem):
    return pltpu.CompilerParams(dimension_semantics=sem, vmem_limit_bytes=V7X_VMEM_LIMIT_BYTES)


def _pick(n, prefs):
    for p in prefs:
        if n % p == 0:
            return p
    return n


def _ada_kernel(c_ref, w_ref, b_ref, o_ref):
    c = c_ref[...]
    s = c * jax.nn.sigmoid(c)
    o_ref[...] = jnp.dot(s, w_ref[...], precision=lax.Precision.HIGHEST,
                         preferred_element_type=F32) + b_ref[...]


def _ada(c8, w, b):
    d, n = w.shape
    tn = _pick(n, (1024, 512, 256, 128))
    return pl.pallas_call(
        _ada_kernel,
        grid=(n // tn,),
        in_specs=[pl.BlockSpec((8, d), lambda j: (0, 0)),
                  pl.BlockSpec((d, tn), lambda j: (0, j)),
                  pl.BlockSpec((1, tn), lambda j: (0, j))],
        out_specs=pl.BlockSpec((8, tn), lambda j: (0, j)),
        out_shape=jax.ShapeDtypeStruct((8, n), F32),
        compiler_params=_cparams(("arbitrary",)),
        name="ada",
    )(c8, w, b.reshape(1, n))


INPROJ_TN = 1024
NORM_CHUNKS = (8, 4, 2, 1)


def _norm_inproj_kernel(starts, nx, x_ref, g_ref, sh_ref, sc_ref, w_ref, *refs):
    o_refs, h_bufs = refs[:-2], refs[-2:]
    i, j = pl.program_id(1), pl.program_id(2)
    tm = h_bufs[0].shape[0]
    xr = tm // nx
    rows = min(tm, MXU_ROWS_PER_DOT)

    def normalise_chunk(h_fill):
        x = x_ref[0]
        ms = jnp.mean(x * x, axis=-1, keepdims=True)
        y = x * lax.rsqrt(ms + EPS) * g_ref[...]
        r0 = pl.multiple_of(jnp.minimum(j, nx - 1) * xr, xr)
        h_fill[pl.ds(r0, xr), :] = (y * (1.0 + sc_ref[0]) + sh_ref[0]).astype(BF16)

    def project(o_ref, h_fill, h_use):
        normalise_chunk(h_fill)
        for r0 in range(0, tm, rows):
            o_ref[0, r0:r0 + rows] = jnp.dot(h_use[r0:r0 + rows], w_ref[...],
                                             preferred_element_type=F32).astype(o_ref.dtype)

    pl.when(i == 0)(functools.partial(normalise_chunk, h_bufs[0]))
    for parity in (0, 1):
        for k, o_ref in enumerate(o_refs):
            pl.when((i > 0) & (i % 2 == parity) & (j >= starts[k]) & (j < starts[k + 1]))(
                functools.partial(project, o_ref, h_bufs[parity], h_bufs[1 - parity]))


def _mod_spec(d, grid_rank, row, j):
    if grid_rank == 2:
        return pl.BlockSpec((1, 1, d), lambda bi, i: (row(bi) * N_MOD + j, 0, 0))
    return pl.BlockSpec((1, 1, d), lambda bi, i, k: (row(bi) * N_MOD + j, 0, 0))


def _norm_inproj(x, g, mod, row, w, outputs):
    b, t, d = x.shape
    tn = INPROJ_TN
    tm = _pick(t, (1024, 512, 256))
    ni = t // tm
    counts = [len(blocks) for _, blocks in outputs]
    starts = [int(v) for v in np.cumsum([0] + counts)]
    nj = starts[-1]
    table = [blk for _, blocks in outputs for blk in blocks]
    nx = next(n for n in NORM_CHUNKS if n <= nj and tm % (16 * n) == 0)
    xr = tm // nx

    def w_block(i, j):
        r = jnp.int32(table[-1])
        for idx in range(len(table) - 2, -1, -1):
            r = jnp.where(j == idx, table[idx], r)
        return jnp.where(i == 0, table[0], r)

    def x_block(i, j):
        return jnp.where(i >= ni, ni * nx - 1, i * nx + jnp.minimum(j, nx - 1))

    def o_spec(k):
        return pl.BlockSpec((1, tm, tn), lambda bi, i, j: (
            bi, jnp.maximum(i - 1, 0), jnp.where(i == 0, 0, jnp.clip(j - starts[k], 0, counts[k] - 1))))

    return pl.pallas_call(
        functools.partial(_norm_inproj_kernel, starts, nx),
        grid=(b, ni + 1, nj),
        in_specs=[pl.BlockSpec((1, xr, d), lambda bi, i, j: (bi, x_block(i, j), 0)),
                  pl.BlockSpec((1, d), lambda bi, i, j: (0, 0)),
                  _mod_spec(d, 3, row, 0), _mod_spec(d, 3, row, 1),
                  pl.BlockSpec((d, tn), lambda bi, i, j: (0, w_block(i, j)))],
        out_specs=[o_spec(k) for k in range(len(outputs))],
        out_shape=[jax.ShapeDtypeStruct((b, t, n * tn), dt) for (dt, _), n in zip(outputs, counts)],
        scratch_shapes=[pltpu.VMEM((tm, d), BF16), pltpu.VMEM((tm, d), BF16)],
        compiler_params=_cparams(("parallel", "arbitrary", "arbitrary")),
        name="norm_inproj",
    )(x, g, mod, mod, w)


def _na_kernel(nb, nr, q_ref, *refs):
    kb_refs, vb_refs, (kc_ref, vc_ref), bias_refs, o_ref = (refs[:nr], refs[nr:2 * nr], refs[2 * nr:2 * nr + 2],
                                                            refs[2 * nr + 2:3 * nr + 2], refs[3 * nr + 2])
    dn = (((1,), (1,)), ((), ()))
    nh, w = NA_SLAB_HEADS, NA_SLAB_HEADS * NA_HEAD_DIM
    hq = nh * GRID_W
    lane_head = lax.broadcasted_iota(jnp.int32, (GRID_W, w), 1) // NA_HEAD_DIM
    own = [lane_head == a for a in range(nh)]
    slabs = range(NA_HEADS // nh)
    work = [(j, g) for j in range(nr) for g in slabs]
    cols = {g: slice(g * w, (g + 1) * w) for g in slabs}
    q = {}
    for j, g in work:
        qg = q_ref[0, j * GRID_W:(j + 1) * GRID_W, cols[g]] * (NA_HEAD_DIM ** -0.5)
        q[j, g] = jnp.concatenate([jnp.where(own[a], qg, jnp.zeros_like(qg)) for a in range(nh)], axis=0)
    s_b = {(j, g): lax.dot_general(q[j, g], kb_refs[j][0, :, cols[g]], dn, preferred_element_type=F32)
           + bias_refs[j][0, g * nh:(g + 1) * nh].reshape(hq, nb) for j, g in work}
    s_c = {}
    for g in slabs:
        both = lax.dot_general(jnp.concatenate([q[j, g] for j in range(nr)], axis=0), kc_ref[0, :, cols[g]], dn,
                               preferred_element_type=F32)
        for j in range(nr):
            s_c[j, g] = both[j * hq:(j + 1) * hq]
    m = {k: jnp.maximum(jnp.max(s_b[k], axis=-1, keepdims=True), jnp.max(s_c[k], axis=-1, keepdims=True))
         for k in work}
    p_b = {k: jnp.exp(s_b[k] - m[k]) for k in work}
    p_c = {k: jnp.exp(s_c[k] - m[k]) for k in work}
    l = {k: jnp.sum(p_b[k], axis=-1, keepdims=True) + jnp.sum(p_c[k], axis=-1, keepdims=True) for k in work}
    o_c = {}
    for g in slabs:
        both = jnp.dot(jnp.concatenate([p_c[j, g] for j in range(nr)], axis=0).astype(BF16), vc_ref[0, :, cols[g]],
                       preferred_element_type=F32)
        for j in range(nr):
            o_c[j, g] = both[j * hq:(j + 1) * hq]
    o = {(j, g): (jnp.dot(p_b[j, g].astype(BF16), vb_refs[j][0, :, cols[g]], preferred_element_type=F32)
                  + o_c[j, g]) / l[j, g] for j, g in work}
    for j, g in work:
        out = o[j, g][0:GRID_W]
        for a in range(1, nh):
            out = jnp.where(own[a], o[j, g][a * GRID_W:(a + 1) * GRID_W], out)
        o_ref[0, j * GRID_W:(j + 1) * GRID_W, cols[g]] = out.astype(BF16)


def _na_bias_table(rpb):
    col = np.arange(GRID_W)
    col_start = np.clip(col - WIN_W // 2, 0, GRID_W - WIN_W)
    in_win = (col[None, :] >= col_start[:, None]) & (col[None, :] < col_start[:, None] + WIN_W)
    dc_idx = np.clip(col[None, :] - col[:, None], 1 - WIN_W, WIN_W - 1) + WIN_W - 1
    pick_col = (dc_idx[None] == np.arange(2 * WIN_W - 1)[:, None, None]).astype(np.float32)
    off, row = np.arange(WIN_H)[:, None, None], np.arange(WIN_H)[None, :, None]
    pick_row = (off + row == np.arange(2 * WIN_H - 1)[None, None, :]).astype(np.float32)
    by_row = jnp.einsum('oid,hdc->ohic', jnp.asarray(pick_row), rpb.astype(F32), precision=lax.Precision.HIGHEST)
    tab = jnp.einsum('ohic,cqk->ohqik', by_row, jnp.asarray(pick_col), precision=lax.Precision.HIGHEST)
    tab = jnp.where(jnp.asarray(in_win)[None, None, :, None, :], tab, NEG_BIG)
    return tab.reshape(WIN_H, NA_HEADS, GRID_W, WIN_H * GRID_W)


def _na(qkv, kv_c, bias_tab):
    b, s, _ = qkv.shape
    lc = kv_c.shape[1]
    rows = s // GRID_W
    nr = _pick(rows, (NA_ROWS_PER_STEP, 1))
    assert rows >= WIN_H
    nb = WIN_H * GRID_W

    def rs_of(r):
        return jnp.clip(r - WIN_H // 2, 0, rows - WIN_H)

    band = (pl.Element(1), pl.Element(nb), pl.Element(NA_WIDTH))
    k_band = [pl.BlockSpec(band, lambda bi, rr, j=j: (bi, rs_of(rr * nr + j) * GRID_W, NA_WIDTH))
              for j in range(nr)]
    v_band = [pl.BlockSpec(band, lambda bi, rr, j=j: (bi, rs_of(rr * nr + j) * GRID_W, 2 * NA_WIDTH))
              for j in range(nr)]
    bias = [pl.BlockSpec((1, NA_HEADS, GRID_W, nb),
                         lambda bi, rr, j=j: (rs_of(rr * nr + j) - (rr * nr + j) + WIN_H - 1, 0, 0, 0))
            for j in range(nr)]
    return pl.pallas_call(
        functools.partial(_na_kernel, nb, nr),
        grid=(b, rows // nr),
        in_specs=[pl.BlockSpec((1, nr * GRID_W, NA_WIDTH), lambda bi, rr: (bi, rr, 0)), *k_band, *v_band,
                  pl.BlockSpec((1, lc, NA_WIDTH), lambda bi, rr: (bi, 0, 0)),
                  pl.BlockSpec((1, lc, NA_WIDTH), lambda bi, rr: (bi, 0, 1)), *bias],
        out_specs=pl.BlockSpec((1, nr * GRID_W, NA_WIDTH), lambda bi, rr: (bi, rr, 0)),
        out_shape=jax.ShapeDtypeStruct((b, s, NA_WIDTH), BF16),
        compiler_params=_cparams(("parallel", "arbitrary")),
        name="na",
    )(qkv, *([qkv] * (2 * nr)), kv_c, kv_c, *([bias_tab] * nr))


def _hg_constants(reverse):
    c = HG_CHUNK
    pos = np.arange(c)
    pmat = np.tril(np.ones((c, c), np.float32))
    qsel = np.zeros((HG_LEVELS, c, HG_DIM), np.float32)
    amask = np.zeros((HG_LEVELS - 1, c, c), np.float32)
    for l in range(HG_LEVELS):
        m = 1 << l
        blk = pos // (2 * m)
        is_q = (pos % (2 * m)) >= m
        qsel[l] = is_q[:, None]
        if l > 0:
            amask[l - 1] = (blk[:, None] == blk[None, :]) & is_q[:, None] & ~is_q[None, :]
    if reverse:
        pmat = pmat[::-1, ::-1]
        qsel = qsel[:, ::-1]
        amask = amask[:, ::-1, ::-1]
    return (jnp.asarray(np.concatenate([pmat, pmat, pmat], axis=1), BF16),
            jnp.asarray(qsel, F32), jnp.asarray(amask, F32))


def _boundary_rows(p, l, reverse):
    c = HG_CHUNK
    m = 1 << l
    off = m if reverse else m - 1
    if 2 * m >= 16:
        return jnp.concatenate([jnp.broadcast_to(p[b0 + off:b0 + off + 1], (2 * m, HG_DIM))
                                for b0 in range(0, c, 2 * m)], axis=0)
    p3 = p.reshape(c // 8, 8, HG_DIM)
    pick = lambda s: jnp.broadcast_to(p3[:, s:s + 1], p3.shape)
    if 2 * m == 8:
        return pick(off).reshape(c, HG_DIM)
    assert 2 * m == 4
    upper = lax.broadcasted_iota(jnp.int32, p3.shape, 1) >= 4
    return jnp.where(upper, pick(4 + off), pick(off)).reshape(c, HG_DIM)


def _chunk_prefixes(pmat, pieces, order):
    c = HG_CHUNK
    rows = lambda a, ci: a[ci * c:(ci + 1) * c]
    pre = {}
    for ca, cb in zip(order[0::2], order[1::2]):
        rhs = jnp.concatenate([jnp.concatenate([rows(a, ca), rows(a, cb)], axis=1) for a in pieces], axis=0)
        both = jnp.dot(pmat, rhs, preferred_element_type=F32)
        pre[ca], pre[cb] = both[:, :HG_DIM], both[:, HG_DIM:]
    if len(order) % 2:
        ci = order[-1]
        pre[ci] = jnp.dot(pmat, jnp.concatenate([rows(a, ci) for a in pieces], axis=0),
                          preferred_element_type=F32)
    return pre


def _split3(a):
    hi = a.astype(BF16)
    r1 = a - hi.astype(F32)
    mid = r1.astype(BF16)
    lo = (r1 - mid.astype(F32)).astype(BF16)
    return hi, mid, lo


def _hg_block(q, z, v, lb, st, pmat, qsel_ref, amask_ref, order, reverse):
    c = HG_CHUNK
    dn_nt = (((1,), (1,)), ((), ()))
    f = lb + (1.0 - lb) * jax.nn.sigmoid(z)
    kk = 1.0 - f
    qf = q.astype(F32)
    rows = lambda a, ci: a[ci * c:(ci + 1) * c]
    pre = _chunk_prefixes(pmat, _split3(jnp.log(f)), order)
    qsel = [qsel_ref[l] != 0.0 for l in range(HG_LEVELS)]
    amask = [amask_ref[l] != 0.0 for l in range(HG_LEVELS - 1)]
    a = {}
    for l in range(HG_VPU_LEVELS, HG_LEVELS):
        sign = jnp.where(qsel[l], LOG2_E, -LOG2_E)
        w = {ci: (jnp.exp2((pre[ci] - _boundary_rows(pre[ci], l, reverse)) * sign)
                  * jnp.where(qsel[l], rows(qf, ci), rows(kk, ci))).astype(BF16) for ci in order}
        p = {ci: lax.dot_general(w[ci], w[ci], dn_nt, preferred_element_type=F32) for ci in order}
        a = {ci: jnp.where(amask[l - 1], p[ci], a[ci] if l > HG_VPU_LEVELS else 0.0) for ci in order}
    last = 0 if reverse else c - 1
    e_in = {ci: jnp.exp(pre[ci]) for ci in order}
    e_out = {ci: jnp.exp(pre[ci][last:last + 1] - pre[ci]) for ci in order}
    upd = {ci: lax.dot_general(rows(v, ci), (rows(kk, ci) * e_out[ci]).astype(BF16), (((0,), (0,)), ((), ())),
                               preferred_element_type=F32) for ci in order}
    o_intra = {ci: jnp.dot(a[ci].astype(BF16), rows(v, ci), preferred_element_type=F32) for ci in order}
    vf = v.astype(F32)
    small = 1 << HG_VPU_LEVELS
    row = lax.broadcasted_iota(jnp.int32, (c, 1), 0)
    in_block = ((c - 1 - row) if reverse else row) % small
    back = lambda x, dist: pltpu.roll(x, (c - dist) if reverse else dist, 0)
    for ci in order:
        qc, kc, vc, fc = rows(qf, ci), rows(kk, ci), rows(vf, ci), rows(f, ci)
        acc = o_intra[ci] + jnp.sum(qc * kc, axis=-1, keepdims=True) * vc
        decay = fc
        for dist in range(1, small):
            if dist > 1:
                decay = decay * back(fc, dist - 1)
            wgt = jnp.sum(qc * decay * back(kc, dist), axis=-1, keepdims=True)
            acc = acc + jnp.where(in_block >= dist, wgt, 0.0) * back(vc, dist)
        o_intra[ci] = acc
    qd = {ci: (rows(qf, ci) * e_in[ci]).astype(BF16) for ci in order}
    out = {}
    for ci in order:
        out[ci] = o_intra[ci] + lax.dot_general(qd[ci], st.astype(BF16), dn_nt, preferred_element_type=F32)
        st = st * e_in[ci][last:last + 1] + upd[ci]
    return out, st


def _hg_kernel(reverse, fuse_readout, nchunk, *refs):
    if fuse_readout:
        (q_ref, v_ref, z_ref, lb_ref, s0_ref, dmat_ref, qsel_ref, amask_ref, of_ref, og_ref, gn_ref,
         o_ref, st_scr) = refs
    else:
        (q_ref, v_ref, z_ref, lb_ref, s0_ref, dmat_ref, qsel_ref, amask_ref,
         o_ref, st_scr) = refs
    t = pl.program_id(2)

    @pl.when(t == 0)
    def _():
        st_scr[...] = s0_ref[0, 0]

    order = list(range(nchunk - 1, -1, -1) if reverse else range(nchunk))
    out, st = _hg_block(q_ref[0], z_ref[0], v_ref[0], lb_ref[0], st_scr[...], dmat_ref[...],
                        qsel_ref, amask_ref, order, reverse)
    o = jnp.concatenate([out[ci] for ci in range(nchunk)], axis=0)
    if fuse_readout:
        o = o + of_ref[0]
        on = o * lax.rsqrt(jnp.mean(o * o, axis=-1, keepdims=True) + EPS) * gn_ref[...]
        g = og_ref[0].astype(F32)
        o_ref[0] = (on * (g * jax.nn.sigmoid(g))).astype(o_ref.dtype)
    else:
        o_ref[0] = o
    st_scr[...] = st


def _hg_state_kernel(reverse, nchunk, v_ref, z_ref, lb_ref, pmat_ref, sout_ref):
    c = HG_CHUNK
    v = v_ref[0]
    f = lb_ref[...] + (1.0 - lb_ref[...]) * jax.nn.sigmoid(z_ref[0])
    kk = 1.0 - f
    pieces = _split3(jnp.log(f))
    order = list(range(nchunk - 1, -1, -1) if reverse else range(nchunk))
    last = 0 if reverse else c - 1
    tile = lambda a, h, ci: a[ci * c:(ci + 1) * c, h * HG_DIM:(h + 1) * HG_DIM]
    pre, upd = {}, {}
    for h in range(HG_HEADS):
        head = [a[:, h * HG_DIM:(h + 1) * HG_DIM] for a in pieces]
        for ci, p in _chunk_prefixes(pmat_ref[...], head, order).items():
            pre[h, ci] = p
    for k, p in pre.items():
        e_out = jnp.exp(p[last:last + 1] - p)
        upd[k] = lax.dot_general(tile(v, *k), (tile(kk, *k) * e_out).astype(BF16), (((0,), (0,)), ((), ())),
                                 preferred_element_type=F32)
    for h in range(HG_HEADS):
        st = jnp.zeros((HG_DIM, HG_DIM), F32)
        for ci in order:
            st = st * jnp.exp(pre[h, ci][last:last + 1]) + upd[h, ci]
        sout_ref[0, h] = st


def _hgrn_state(val, fg, lb, dmat, reverse):
    b, l, _ = val.shape
    assert l % HG_CHUNK == 0
    fblk = 1 if reverse else 0
    return pl.pallas_call(
        functools.partial(_hg_state_kernel, reverse, l // HG_CHUNK),
        grid=(b,),
        in_specs=[pl.BlockSpec((1, l, HG_WIDTH), lambda bi: (bi, 0, 0)),
                  pl.BlockSpec((1, l, HG_WIDTH), lambda bi: (bi, 0, fblk)),
                  pl.BlockSpec((1, HG_WIDTH), lambda bi: (0, 0)),
                  pl.BlockSpec(dmat.shape, lambda bi: (0, 0))],
        out_specs=pl.BlockSpec((1, HG_HEADS, HG_DIM, HG_DIM), lambda bi: (bi, 0, 0, 0)),
        out_shape=jax.ShapeDtypeStruct((b, HG_HEADS, HG_DIM, HG_DIM), F32),
        compiler_params=_cparams(("parallel",)),
        name="hgrn_state_bwd" if reverse else "hgrn_state_fwd",
    )(val, fg, lb.reshape(1, HG_WIDTH), dmat)


def _hgrn(mix, fg, lb, s0, consts, reverse, d, o_fwd=None, gn=None):
    b, s, _ = mix.shape
    tt = _pick(s, (2048, 1024, 512, 256, 128, 64))
    nt = s // tt
    base = 2 * d // HG_DIM
    fuse = o_fwd is not None
    dmat, qsel, amask = consts
    fcol = HG_HEADS if reverse else 0

    def tmap(t):
        return nt - 1 - t if reverse else t

    blk = (1, tt, HG_DIM)
    in_specs = [pl.BlockSpec(blk, lambda bi, h, t: (bi, tmap(t), base + h)),
                pl.BlockSpec(blk, lambda bi, h, t: (bi, tmap(t), base + HG_HEADS + h)),
                pl.BlockSpec(blk, lambda bi, h, t: (bi, tmap(t), fcol + h)),
                pl.BlockSpec((1, 1, HG_DIM), lambda bi, h, t: (h, 0, 0)),
                pl.BlockSpec((1, 1, HG_DIM, HG_DIM), lambda bi, h, t: (bi, h, 0, 0)),
                pl.BlockSpec(dmat.shape, lambda bi, h, t: (0, 0)),
                pl.BlockSpec(qsel.shape, lambda bi, h, t: (0, 0, 0)),
                pl.BlockSpec(amask.shape, lambda bi, h, t: (0, 0, 0))]
    args = [mix, mix, fg, lb.reshape(HG_HEADS, 1, HG_DIM), s0, dmat, qsel, amask]
    if fuse:
        in_specs += [pl.BlockSpec(blk, lambda bi, h, t: (bi, tmap(t), h)),
                     pl.BlockSpec(blk, lambda bi, h, t: (bi, tmap(t), base + 2 * HG_HEADS + h)),
                     pl.BlockSpec((1, HG_DIM), lambda bi, h, t: (0, 0))]
        args += [o_fwd, mix, gn.reshape(1, HG_DIM)]
    return pl.pallas_call(
        functools.partial(_hg_kernel, reverse, fuse, tt // HG_CHUNK),
        grid=(b, HG_HEADS, nt),
        in_specs=in_specs,
        out_specs=pl.BlockSpec(blk, lambda bi, h, t: (bi, tmap(t), h)),
        out_shape=jax.ShapeDtypeStruct((b, s, HG_WIDTH), BF16 if fuse else F32),
        scratch_shapes=[pltpu.VMEM((HG_DIM, HG_DIM), F32)],
        compiler_params=_cparams(("parallel", "parallel", "arbitrary")),
        name="hgrn_bwd" if reverse else "hgrn_fwd",
    )(*args)


def _merge_kernel(oa_ref, ob_ref, ga_ref, gb_ref, x_ref, g1_ref, n2_ref, sh2_ref, sc2_ref,
                  wpa_ref, wpb_ref, wo_ref, x1_ref, h2_ref, y_scr):
    tm, d = x1_ref.shape[1:]
    nc = _pick(d, (512, 256, 128))
    oa, ob = oa_ref[0], ob_ref[0]
    for n0 in range(0, d, nc):
        cs = slice(n0, n0 + nc)
        ya = jnp.dot(oa, wpa_ref[:, cs], preferred_element_type=F32)
        yb = jnp.dot(ob, wpb_ref[:, cs], preferred_element_type=F32)
        y_scr[:, cs] = (jax.nn.sigmoid(ga_ref[0, :, cs].astype(F32)) * ya
                        + jax.nn.sigmoid(gb_ref[0, :, cs].astype(F32)) * yb).astype(BF16)
    y = y_scr[...]
    ss = jnp.zeros((tm, 1), F32)
    for n0 in range(0, d, nc):
        cs = slice(n0, n0 + nc)
        x1 = x_ref[0, :, cs] + g1_ref[0, :, cs] * jnp.dot(y, wo_ref[:, cs], preferred_element_type=F32)
        x1_ref[0, :, cs] = x1
        ss = ss + jnp.sum(x1 * x1, axis=-1, keepdims=True)
    r = lax.rsqrt(ss * (1.0 / d) + EPS)
    h2_ref[0] = (x1_ref[0] * r * n2_ref[...] * (1.0 + sc2_ref[0]) + sh2_ref[0]).astype(BF16)


def _merge(o_a, o_b, mix, x, mod, n2, wpa, wpb, wo):
    b, s, d = x.shape
    tm = _pick(s, (512, 256, 128))
    const = lambda shape: pl.BlockSpec(shape, lambda bi, i: (0,) * len(shape), pipeline_mode=pl.Buffered(1))
    vec = lambda j: _mod_spec(d, 2, lambda bi: bi, j)
    return pl.pallas_call(
        _merge_kernel,
        grid=(b, s // tm),
        in_specs=[pl.BlockSpec((1, tm, NA_WIDTH), lambda bi, i: (bi, i, 0)),
                  pl.BlockSpec((1, tm, HG_WIDTH), lambda bi, i: (bi, i, 0)),
                  pl.BlockSpec((1, tm, d), lambda bi, i: (bi, i, 0)),
                  pl.BlockSpec((1, tm, d), lambda bi, i: (bi, i, 1)),
                  pl.BlockSpec((1, tm, d), lambda bi, i: (bi, i, 0)),
                  vec(2), const((1, d)), vec(3), vec(4),
                  const(wpa.shape), const(wpb.shape), const(wo.shape)],
        out_specs=[pl.BlockSpec((1, tm, d), lambda bi, i: (bi, i, 0)),
                   pl.BlockSpec((1, tm, d), lambda bi, i: (bi, i, 0))],
        out_shape=[jax.ShapeDtypeStruct((b, s, d), F32), jax.ShapeDtypeStruct((b, s, d), BF16)],
        scratch_shapes=[pltpu.VMEM((tm, d), BF16)],
        compiler_params=_cparams(("parallel", "parallel")),
        name="merge",
    )(o_a, o_b, mix, mix, x, mod, n2, mod, mod, wpa, wpb, wo)


def _ffn_kernel(nf, nx, h_ref, x1c_ref, g2_ref, fg_ref, wa_ref, wu_ref, wo_ref, o_ref, x1_scr):
    j = pl.program_id(2)
    tm, d = o_ref.shape[1:]
    rows = min(tm, MXU_ROWS_PER_DOT)
    nc = _pick(d, (512, 256, 128))
    xr = tm // nx

    @pl.when(j < nx)
    def _():
        x1_scr[pl.ds(pl.multiple_of(j * xr, xr), xr), :] = x1c_ref[0]

    def hidden_slice(first, last):
        for r0 in range(0, tm, rows):
            rs = slice(r0, r0 + rows)
            h = h_ref[0, rs]
            a = jnp.dot(h, wa_ref[...], preferred_element_type=F32)
            u = jnp.dot(h, wu_ref[...], preferred_element_type=F32)
            gate = (a * jax.nn.sigmoid(a) * u).astype(BF16)
            for n0 in range(0, d, nc):
                part = jnp.dot(gate, wo_ref[:, n0:n0 + nc], preferred_element_type=F32)
                if first:
                    o_ref[0, rs, n0:n0 + nc] = part
                else:
                    o_ref[0, rs, n0:n0 + nc] += part
            if last:
                x2 = x1_scr[rs, :] + g2_ref[0] * o_ref[0, rs]
                ms = jnp.mean(x2 * x2, axis=-1, keepdims=True)
                o_ref[0, rs] = x2 * lax.rsqrt(ms + EPS) * fg_ref[...]

    if nf == 1:
        hidden_slice(True, True)
    else:
        pl.when(j == 0)(functools.partial(hidden_slice, True, False))
        if nf > 2:
            pl.when((j > 0) & (j < nf - 1))(functools.partial(hidden_slice, False, False))
        pl.when(j == nf - 1)(functools.partial(hidden_slice, False, True))


def _ffn(h2, x1, mod, final_g, w_in, w_out):
    b, s, d = x1.shape
    fh = w_out.shape[0]
    tf = _pick(fh, (512, 256, 128))
    nf = fh // tf
    tm = _pick(s, (1024, 512, 256, 128))
    nx = max(n for n in (1, 2, 4, 8) if n <= nf and tm % (8 * n) == 0)
    return pl.pallas_call(
        functools.partial(_ffn_kernel, nf, nx),
        grid=(b, s // tm, nf),
        in_specs=[pl.BlockSpec((1, tm, d), lambda bi, i, j: (bi, i, 0)),
                  pl.BlockSpec((1, tm // nx, d), lambda bi, i, j: (bi, i * nx + jnp.minimum(j, nx - 1), 0)),
                  _mod_spec(d, 3, lambda bi: bi, 5),
                  pl.BlockSpec((1, d), lambda bi, i, j: (0, 0)),
                  pl.BlockSpec((d, tf), lambda bi, i, j: (0, j)),
                  pl.BlockSpec((d, tf), lambda bi, i, j: (0, nf + j)),
                  pl.BlockSpec((tf, d), lambda bi, i, j: (j, 0))],
        out_specs=pl.BlockSpec((1, tm, d), lambda bi, i, j: (bi, i, 0)),
        out_shape=jax.ShapeDtypeStruct((b, s, d), F32),
        scratch_shapes=[pltpu.VMEM((tm, d), F32)],
        compiler_params=_cparams(("parallel", "parallel", "arbitrary")),
        name="ffn",
    )(h2, x1, mod, final_g.reshape(1, d), w_in, w_in, w_out)


def kernel(x, c, ctx, c_ctx, w_ada, b_ada, norm1_g, w_in, na_rpb, hg_lb_logits, hg_norm_g,
           w_pa, w_pb, w_out, norm2_g, w_ffn_in, w_ffn_out, final_g):
    b, s, d = x.shape
    assert w_ada.shape[0] == 1, "single layer"

    lb_table = jnp.cumsum(jax.nn.softmax(hg_lb_logits.astype(F32), axis=0), axis=0)
    lb_f, lb_b = lb_table[0, 0], lb_table[0, 1]

    c8 = jnp.zeros((8, d), F32).at[:b].set(c).at[b].set(c_ctx)
    mod = _ada(c8, w_ada[0], b_ada[0]).reshape(8 * N_MOD, 1, d)

    assert NA_WIDTH == HG_WIDTH == INPROJ_TN and (2 * d) % INPROJ_TN == 0
    q_, k_, v_, hq_, ff_, fb_, hi_, hog_ = range(8)
    gates = list(range(8, 8 + 2 * d // INPROJ_TN))
    w_bf = w_in[0].astype(BF16)
    n1 = norm1_g[0].reshape(1, d)

    qkv, mix, fg = _norm_inproj(x, n1, mod, lambda bi: bi, w_bf,
                                [(BF16, [q_, k_, v_]), (BF16, gates + [hq_, hi_, hog_]), (F32, [ff_, fb_])])
    lc = ctx.shape[1]
    kv_c, hi_c, fg_c = [a.reshape(b, lc, -1) for a in _norm_inproj(
        ctx.reshape(1, b * lc, d), n1, mod, lambda bi: b, w_bf,
        [(BF16, [k_, v_]), (BF16, [hi_]), (F32, [ff_, fb_])])]

    o_na = _na(qkv, kv_c, _na_bias_table(na_rpb[0]))

    cf, cb = _hg_constants(False), _hg_constants(True)
    s_f = _hgrn_state(hi_c, fg_c, lb_f, cf[0], False)
    s_b = _hgrn_state(hi_c, fg_c, lb_b, cb[0], True)
    o_f = _hgrn(mix, fg, lb_f, s_f, cf, False, d)
    o_hg = _hgrn(mix, fg, lb_b, s_b, cb, True, d, o_fwd=o_f, gn=hg_norm_g[0])

    x1, h2 = _merge(o_na, o_hg, mix, x, mod, norm2_g[0].reshape(1, d),
                    w_pa[0].astype(BF16), w_pb[0].astype(BF16), w_out[0].astype(BF16))
    return _ffn(h2, x1, mod, final_g, w_ffn_in[0].astype(BF16), w_ffn_out[0].astype(BF16))
```

```python
import functools

import jax
import jax.numpy as jnp
import numpy as np
from jax import lax
from jax.experimental import pallas as pl
from jax.experimental.pallas import tpu as pltpu

F32 = jnp.float32
BF16 = jnp.bfloat16

GRID_W = 64
WIN_H = 8
WIN_W = 16
NA_HEADS = 16
NA_HEAD_DIM = 64
NA_WIDTH = NA_HEADS * NA_HEAD_DIM
HG_HEADS = 8
HG_DIM = 128
HG_WIDTH = HG_HEADS * HG_DIM
N_MOD = 6
EPS = 1e-6
NEG_BIG = -1e30

V7X_VMEM_LIMIT_BYTES = 60000 * 1024
MXU_ROWS_PER_DOT = 512
NA_SLAB_HEADS = 4
NA_ROWS_PER_STEP = 2
HG_CHUNK = 64
HG_LEVELS = 6
HG_VPU_LEVELS = 1
LOG2_E = 1.4426950408889634


def _cparams(sem):
    return pltpu.CompilerParams(dimension_semantics=sem, vmem_limit_bytes=V7X_VMEM_LIMIT_BYTES)


def _pick(n, prefs):
    for p in prefs:
        if n % p == 0:
            return p
    return n


def _ada_kernel(c_ref, w_ref, b_ref, o_ref):
    c = c_ref[...]
    s = c * jax.nn.sigmoid(c)
    s3 = jnp.concatenate([p.astype(F32) for p in _split3(s)], axis=0).astype(BF16)
    w = w_ref[...]
    w_hi = w.astype(BF16)
    w_lo = (w - w_hi.astype(F32)).astype(BF16)
    acc = (jnp.dot(s3, w_hi, preferred_element_type=F32) + jnp.dot(s3, w_lo, preferred_element_type=F32))
    o_ref[...] = acc[0:8] + acc[8:16] + acc[16:24] + b_ref[...]


def _ada(c8, w, b):
    d, n = w.shape
    tn = _pick(n, (1024, 512, 256, 128))
    return pl.pallas_call(
        _ada_kernel,
        grid=(n // tn,),
        in_specs=[pl.BlockSpec((8, d), lambda j: (0, 0)),
                  pl.BlockSpec((d, tn), lambda j: (0, j)),
                  pl.BlockSpec((1, tn), lambda j: (0, j))],
        out_specs=pl.BlockSpec((8, tn), lambda j: (0, j)),
        out_shape=jax.ShapeDtypeStruct((8, n), F32),
        compiler_params=_cparams(("arbitrary",)),
        name="ada",
    )(c8, w, b.reshape(1, n))


INPROJ_TN = 1024
NORM_CHUNKS = (8, 4, 2, 1)


def _norm_inproj_kernel(starts, nx, x_ref, g_ref, sh_ref, sc_ref, w_ref, *refs):
    o_refs, h_bufs = refs[:-2], refs[-2:]
    i, j = pl.program_id(1), pl.program_id(2)
    tm = h_bufs[0].shape[0]
    xr = tm // nx
    rows = min(tm, MXU_ROWS_PER_DOT)

    def normalise_chunk(h_fill):
        x = x_ref[0]
        ms = jnp.mean(x * x, axis=-1, keepdims=True)
        y = x * lax.rsqrt(ms + EPS) * g_ref[...]
        r0 = pl.multiple_of(jnp.minimum(j, nx - 1) * xr, xr)
        h_fill[pl.ds(r0, xr), :] = (y * (1.0 + sc_ref[0]) + sh_ref[0]).astype(BF16)

    def project(o_ref, h_fill, h_use):
        normalise_chunk(h_fill)
        for r0 in range(0, tm, rows):
            o_ref[0, r0:r0 + rows] = jnp.dot(h_use[r0:r0 + rows], w_ref[...],
                                             preferred_element_type=F32).astype(o_ref.dtype)

    pl.when(i == 0)(functools.partial(normalise_chunk, h_bufs[0]))
    for parity in (0, 1):
        for k, o_ref in enumerate(o_refs):
            pl.when((i > 0) & (i % 2 == parity) & (j >= starts[k]) & (j < starts[k + 1]))(
                functools.partial(project, o_ref, h_bufs[parity], h_bufs[1 - parity]))


def _mod_spec(d, grid_rank, row, j):
    if grid_rank == 2:
        return pl.BlockSpec((1, 1, d), lambda bi, i: (row(bi) * N_MOD + j, 0, 0))
    return pl.BlockSpec((1, 1, d), lambda bi, i, k: (row(bi) * N_MOD + j, 0, 0))


def _norm_inproj(x, g, mod, row, w, outputs):
    b, t, d = x.shape
    tn = INPROJ_TN
    tm = _pick(t, (1024, 512, 256))
    ni = t // tm
    counts = [len(blocks) for _, blocks in outputs]
    starts = [int(v) for v in np.cumsum([0] + counts)]
    nj = starts[-1]
    table = [blk for _, blocks in outputs for blk in blocks]
    nx = next(n for n in NORM_CHUNKS if n <= nj and tm % (16 * n) == 0)
    xr = tm // nx

    def w_block(i, j):
        r = jnp.int32(table[-1])
        for idx in range(len(table) - 2, -1, -1):
            r = jnp.where(j == idx, table[idx], r)
        return jnp.where(i == 0, table[0], r)

    def x_block(i, j):
        return jnp.where(i >= ni, ni * nx - 1, i * nx + jnp.minimum(j, nx - 1))

    def o_spec(k):
        return pl.BlockSpec((1, tm, tn), lambda bi, i, j: (
            bi, jnp.maximum(i - 1, 0), jnp.where(i == 0, 0, jnp.clip(j - starts[k], 0, counts[k] - 1))))

    return pl.pallas_call(
        functools.partial(_norm_inproj_kernel, starts, nx),
        grid=(b, ni + 1, nj),
        in_specs=[pl.BlockSpec((1, xr, d), lambda bi, i, j: (bi, x_block(i, j), 0)),
                  pl.BlockSpec((1, d), lambda bi, i, j: (0, 0)),
                  _mod_spec(d, 3, row, 0), _mod_spec(d, 3, row, 1),
                  pl.BlockSpec((d, tn), lambda bi, i, j: (0, w_block(i, j)))],
        out_specs=[o_spec(k) for k in range(len(outputs))],
        out_shape=[jax.ShapeDtypeStruct((b, t, n * tn), dt) for (dt, _), n in zip(outputs, counts)],
        scratch_shapes=[pltpu.VMEM((tm, d), BF16), pltpu.VMEM((tm, d), BF16)],
        compiler_params=_cparams(("parallel", "arbitrary", "arbitrary")),
        name="norm_inproj",
    )(x, g, mod, mod, w)


def _na_kernel(nb, nr, q_ref, *refs):
    kb_refs, vb_refs, (kc_ref, vc_ref), bias_refs, o_ref = (refs[:nr], refs[nr:2 * nr], refs[2 * nr:2 * nr + 2],
                                                            refs[2 * nr + 2:3 * nr + 2], refs[3 * nr + 2])
    dn = (((1,), (1,)), ((), ()))
    nh, w = NA_SLAB_HEADS, NA_SLAB_HEADS * NA_HEAD_DIM
    hq = nh * GRID_W
    lane_head = lax.broadcasted_iota(jnp.int32, (GRID_W, w), 1) // NA_HEAD_DIM
    own = [lane_head == a for a in range(nh)]
    slabs = range(NA_HEADS // nh)
    work = [(j, g) for j in range(nr) for g in slabs]
    cols = {g: slice(g * w, (g + 1) * w) for g in slabs}
    q = {}
    for j, g in work:
        qg = q_ref[0, j * GRID_W:(j + 1) * GRID_W, cols[g]] * (NA_HEAD_DIM ** -0.5)
        q[j, g] = jnp.concatenate([jnp.where(own[a], qg, jnp.zeros_like(qg)) for a in range(nh)], axis=0)
    s_b = {(j, g): lax.dot_general(q[j, g], kb_refs[j][0, :, cols[g]], dn, preferred_element_type=F32)
           + bias_refs[j][0, g * nh:(g + 1) * nh].reshape(hq, nb) for j, g in work}
    s_c = {}
    for g in slabs:
        both = lax.dot_general(jnp.concatenate([q[j, g] for j in range(nr)], axis=0), kc_ref[0, :, cols[g]], dn,
                               preferred_element_type=F32)
        for j in range(nr):
            s_c[j, g] = both[j * hq:(j + 1) * hq]
    m = {k: jnp.maximum(jnp.max(s_b[k], axis=-1, keepdims=True), jnp.max(s_c[k], axis=-1, keepdims=True))
         for k in work}
    p_b = {k: jnp.exp(s_b[k] - m[k]) for k in work}
    p_c = {k: jnp.exp(s_c[k] - m[k]) for k in work}
    l = {k: jnp.sum(p_b[k], axis=-1, keepdims=True) + jnp.sum(p_c[k], axis=-1, keepdims=True) for k in work}
    o_c = {}
    for g in slabs:
        both = jnp.dot(jnp.concatenate([p_c[j, g] for j in range(nr)], axis=0).astype(BF16), vc_ref[0, :, cols[g]],
                       preferred_element_type=F32)
        for j in range(nr):
            o_c[j, g] = both[j * hq:(j + 1) * hq]
    o = {(j, g): (jnp.dot(p_b[j, g].astype(BF16), vb_refs[j][0, :, cols[g]], preferred_element_type=F32)
                  + o_c[j, g]) / l[j, g] for j, g in work}
    for j, g in work:
        out = o[j, g][0:GRID_W]
        for a in range(1, nh):
            out = jnp.where(own[a], o[j, g][a * GRID_W:(a + 1) * GRID_W], out)
        o_ref[0, j * GRID_W:(j + 1) * GRID_W, cols[g]] = out.astype(BF16)


def _na_bias_table(rpb):
    col = np.arange(GRID_W)
    col_start = np.clip(col - WIN_W // 2, 0, GRID_W - WIN_W)
    in_win = (col[None, :] >= col_start[:, None]) & (col[None, :] < col_start[:, None] + WIN_W)
    dc_idx = np.clip(col[None, :] - col[:, None], 1 - WIN_W, WIN_W - 1) + WIN_W - 1
    ncol = 2 * WIN_W - 1
    pick_col = (np.where(in_win, dc_idx, ncol)[None] == np.arange(ncol + 1)[:, None, None]).astype(np.float32)
    off, row = np.arange(WIN_H)[:, None, None], np.arange(WIN_H)[None, :, None]
    pick_row = (off + row == np.arange(2 * WIN_H - 1)[None, None, :]).astype(np.float32)
    rpb_m = jnp.concatenate([rpb.astype(F32), jnp.full(rpb.shape[:2] + (1,), NEG_BIG, F32)], axis=-1)
    by_row = jnp.einsum('oid,hdc->ohic', jnp.asarray(pick_row), rpb_m, precision=lax.Precision.HIGHEST)
    tab = jnp.einsum('ohic,cqk->ohqik', by_row, jnp.asarray(pick_col), precision=lax.Precision.HIGHEST)
    return tab.reshape(WIN_H, NA_HEADS, GRID_W, WIN_H * GRID_W)


def _na(qkv, kv_c, bias_tab):
    b, s, _ = qkv.shape
    lc = kv_c.shape[1]
    rows = s // GRID_W
    nr = _pick(rows, (NA_ROWS_PER_STEP, 1))
    assert rows >= WIN_H
    nb = WIN_H * GRID_W

    def rs_of(r):
        return jnp.clip(r - WIN_H // 2, 0, rows - WIN_H)

    band = (pl.Element(1), pl.Element(nb), pl.Element(NA_WIDTH))
    k_band = [pl.BlockSpec(band, lambda bi, rr, j=j: (bi, rs_of(rr * nr + j) * GRID_W, NA_WIDTH))
              for j in range(nr)]
    v_band = [pl.BlockSpec(band, lambda bi, rr, j=j: (bi, rs_of(rr * nr + j) * GRID_W, 2 * NA_WIDTH))
              for j in range(nr)]
    bias = [pl.BlockSpec((1, NA_HEADS, GRID_W, nb),
                         lambda bi, rr, j=j: (rs_of(rr * nr + j) - (rr * nr + j) + WIN_H - 1, 0, 0, 0))
            for j in range(nr)]
    return pl.pallas_call(
        functools.partial(_na_kernel, nb, nr),
        grid=(b, rows // nr),
        in_specs=[pl.BlockSpec((1, nr * GRID_W, NA_WIDTH), lambda bi, rr: (bi, rr, 0)), *k_band, *v_band,
                  pl.BlockSpec((1, lc, NA_WIDTH), lambda bi, rr: (bi, 0, 0)),
                  pl.BlockSpec((1, lc, NA_WIDTH), lambda bi, rr: (bi, 0, 1)), *bias],
        out_specs=pl.BlockSpec((1, nr * GRID_W, NA_WIDTH), lambda bi, rr: (bi, rr, 0)),
        out_shape=jax.ShapeDtypeStruct((b, s, NA_WIDTH), BF16),
        compiler_params=_cparams(("parallel", "arbitrary")),
        name="na",
    )(qkv, *([qkv] * (2 * nr)), kv_c, kv_c, *([bias_tab] * nr))


def _hg_constants(reverse):
    c = HG_CHUNK
    pos = np.arange(c)
    pmat = np.tril(np.ones((c, c), np.float32))
    qsel = np.zeros((HG_LEVELS, c, HG_DIM), np.float32)
    amask = np.zeros((HG_LEVELS - 1, c, c), np.float32)
    for l in range(HG_LEVELS):
        m = 1 << l
        blk = pos // (2 * m)
        is_q = (pos % (2 * m)) >= m
        qsel[l] = is_q[:, None]
        if l > 0:
            amask[l - 1] = (blk[:, None] == blk[None, :]) & is_q[:, None] & ~is_q[None, :]
    if reverse:
        pmat = pmat[::-1, ::-1]
        qsel = qsel[:, ::-1]
        amask = amask[:, ::-1, ::-1]
    return (jnp.asarray(np.concatenate([pmat, pmat, pmat], axis=1), BF16),
            jnp.asarray(qsel, F32), jnp.asarray(amask, F32))


def _boundary_rows(p, l, reverse):
    c = HG_CHUNK
    m = 1 << l
    off = m if reverse else m - 1
    if 2 * m >= 16:
        return jnp.concatenate([jnp.broadcast_to(p[b0 + off:b0 + off + 1], (2 * m, HG_DIM))
                                for b0 in range(0, c, 2 * m)], axis=0)
    p3 = p.reshape(c // 8, 8, HG_DIM)
    pick = lambda s: jnp.broadcast_to(p3[:, s:s + 1], p3.shape)
    if 2 * m == 8:
        return pick(off).reshape(c, HG_DIM)
    assert 2 * m == 4
    upper = lax.broadcasted_iota(jnp.int32, p3.shape, 1) >= 4
    return jnp.where(upper, pick(4 + off), pick(off)).reshape(c, HG_DIM)


def _chunk_prefixes(pmat, pieces, order):
    c = HG_CHUNK
    rows = lambda a, ci: a[ci * c:(ci + 1) * c]
    pre = {}
    for ca, cb in zip(order[0::2], order[1::2]):
        rhs = jnp.concatenate([jnp.concatenate([rows(a, ca), rows(a, cb)], axis=1) for a in pieces], axis=0)
        both = jnp.dot(pmat, rhs, preferred_element_type=F32)
        pre[ca], pre[cb] = both[:, :HG_DIM], both[:, HG_DIM:]
    if len(order) % 2:
        ci = order[-1]
        pre[ci] = jnp.dot(pmat, jnp.concatenate([rows(a, ci) for a in pieces], axis=0),
                          preferred_element_type=F32)
    return pre


def _split3(a):
    hi = a.astype(BF16)
    r1 = a - hi.astype(F32)
    mid = r1.astype(BF16)
    lo = (r1 - mid.astype(F32)).astype(BF16)
    return hi, mid, lo


def _hg_block(q, z, v, lb, st, pmat, qsel_ref, amask_ref, order, reverse):
    c = HG_CHUNK
    dn_nt = (((1,), (1,)), ((), ()))
    f = lb + (1.0 - lb) * jax.nn.sigmoid(z)
    kk = 1.0 - f
    qf = q.astype(F32)
    rows = lambda a, ci: a[ci * c:(ci + 1) * c]
    pre = _chunk_prefixes(pmat, _split3(jnp.log(f)), order)
    qsel = [qsel_ref[l] != 0.0 for l in range(HG_LEVELS)]
    amask = [amask_ref[l] != 0.0 for l in range(HG_LEVELS - 1)]
    a = {}
    for l in range(HG_VPU_LEVELS, HG_LEVELS):
        sign = jnp.where(qsel[l], LOG2_E, -LOG2_E)
        w = {ci: (jnp.exp2((pre[ci] - _boundary_rows(pre[ci], l, reverse)) * sign)
                  * jnp.where(qsel[l], rows(qf, ci), rows(kk, ci))).astype(BF16) for ci in order}
        p = {ci: lax.dot_general(w[ci], w[ci], dn_nt, preferred_element_type=F32) for ci in order}
        a = {ci: jnp.where(amask[l - 1], p[ci], a[ci] if l > HG_VPU_LEVELS else 0.0) for ci in order}
    last = 0 if reverse else c - 1
    e_in = {ci: jnp.exp(pre[ci]) for ci in order}
    e_out = {ci: jnp.exp(pre[ci][last:last + 1] - pre[ci]) for ci in order}
    upd = {ci: lax.dot_general(rows(v, ci), (rows(kk, ci) * e_out[ci]).astype(BF16), (((0,), (0,)), ((), ())),
                               preferred_element_type=F32) for ci in order}
    o_intra = {ci: jnp.dot(a[ci].astype(BF16), rows(v, ci), preferred_element_type=F32) for ci in order}
    vf = v.astype(F32)
    small = 1 << HG_VPU_LEVELS
    row = lax.broadcasted_iota(jnp.int32, (c, 1), 0)
    in_block = ((c - 1 - row) if reverse else row) % small
    back = lambda x, dist: pltpu.roll(x, (c - dist) if reverse else dist, 0)
    for ci in order:
        qc, kc, vc, fc = rows(qf, ci), rows(kk, ci), rows(vf, ci), rows(f, ci)
        acc = o_intra[ci] + jnp.sum(qc * kc, axis=-1, keepdims=True) * vc
        decay = fc
        for dist in range(1, small):
            if dist > 1:
                decay = decay * back(fc, dist - 1)
            wgt = jnp.sum(qc * decay * back(kc, dist), axis=-1, keepdims=True)
            acc = acc + jnp.where(in_block >= dist, wgt, 0.0) * back(vc, dist)
        o_intra[ci] = acc
    qd = {ci: (rows(qf, ci) * e_in[ci]).astype(BF16) for ci in order}
    out = {}
    for ci in order:
        out[ci] = o_intra[ci] + lax.dot_general(qd[ci], st.astype(BF16), dn_nt, preferred_element_type=F32)
        st = st * e_in[ci][last:last + 1] + upd[ci]
    return out, st


def _hg_kernel(reverse, fuse_readout, nchunk, *refs):
    if fuse_readout:
        (q_ref, v_ref, z_ref, lb_ref, s0_ref, dmat_ref, qsel_ref, amask_ref, of_ref, og_ref, gn_ref,
         o_ref, st_scr) = refs
    else:
        (q_ref, v_ref, z_ref, lb_ref, s0_ref, dmat_ref, qsel_ref, amask_ref,
         o_ref, st_scr) = refs
    t = pl.program_id(2)

    @pl.when(t == 0)
    def _():
        st_scr[...] = s0_ref[0, 0]

    order = list(range(nchunk - 1, -1, -1) if reverse else range(nchunk))
    out, st = _hg_block(q_ref[0], z_ref[0], v_ref[0], lb_ref[0], st_scr[...], dmat_ref[...],
                        qsel_ref, amask_ref, order, reverse)
    o = jnp.concatenate([out[ci] for ci in range(nchunk)], axis=0)
    if fuse_readout:
        o = o + of_ref[0]
        on = o * lax.rsqrt(jnp.mean(o * o, axis=-1, keepdims=True) + EPS) * gn_ref[...]
        g = og_ref[0].astype(F32)
        o_ref[0] = (on * (g * jax.nn.sigmoid(g))).astype(o_ref.dtype)
    else:
        o_ref[0] = o
    st_scr[...] = st


def _hg_state_kernel(reverse, nchunk, v_ref, z_ref, lb_ref, pmat_ref, sout_ref):
    c = HG_CHUNK
    v = v_ref[0]
    f = lb_ref[...] + (1.0 - lb_ref[...]) * jax.nn.sigmoid(z_ref[0])
    kk = 1.0 - f
    pieces = _split3(jnp.log(f))
    order = list(range(nchunk - 1, -1, -1) if reverse else range(nchunk))
    last = 0 if reverse else c - 1
    tile = lambda a, h, ci: a[ci * c:(ci + 1) * c, h * HG_DIM:(h + 1) * HG_DIM]
    pre, upd = {}, {}
    for h in range(HG_HEADS):
        head = [a[:, h * HG_DIM:(h + 1) * HG_DIM] for a in pieces]
        for ci, p in _chunk_prefixes(pmat_ref[...], head, order).items():
            pre[h, ci] = p
    for k, p in pre.items():
        e_out = jnp.exp(p[last:last + 1] - p)
        upd[k] = lax.dot_general(tile(v, *k), (tile(kk, *k) * e_out).astype(BF16), (((0,), (0,)), ((), ())),
                                 preferred_element_type=F32)
    for h in range(HG_HEADS):
        st = jnp.zeros((HG_DIM, HG_DIM), F32)
        for ci in order:
            st = st * jnp.exp(pre[h, ci][last:last + 1]) + upd[h, ci]
        sout_ref[0, h] = st


def _hgrn_state(val, fg, lb, dmat, reverse):
    b, l, _ = val.shape
    assert l % HG_CHUNK == 0
    fblk = 1 if reverse else 0
    return pl.pallas_call(
        functools.partial(_hg_state_kernel, reverse, l // HG_CHUNK),
        grid=(b,),
        in_specs=[pl.BlockSpec((1, l, HG_WIDTH), lambda bi: (bi, 0, 0)),
                  pl.BlockSpec((1, l, HG_WIDTH), lambda bi: (bi, 0, fblk)),
                  pl.BlockSpec((1, HG_WIDTH), lambda bi: (0, 0)),
                  pl.BlockSpec(dmat.shape, lambda bi: (0, 0))],
        out_specs=pl.BlockSpec((1, HG_HEADS, HG_DIM, HG_DIM), lambda bi: (bi, 0, 0, 0)),
        out_shape=jax.ShapeDtypeStruct((b, HG_HEADS, HG_DIM, HG_DIM), F32),
        compiler_params=_cparams(("parallel",)),
        name="hgrn_state_bwd" if reverse else "hgrn_state_fwd",
    )(val, fg, lb.reshape(1, HG_WIDTH), dmat)


def _hgrn(mix, fg, lb, s0, consts, reverse, d, o_fwd=None, gn=None):
    b, s, _ = mix.shape
    tt = _pick(s, (2048, 1024, 512, 256, 128, 64))
    nt = s // tt
    base = 2 * d // HG_DIM
    fuse = o_fwd is not None
    dmat, qsel, amask = consts
    fcol = HG_HEADS if reverse else 0

    def tmap(t):
        return nt - 1 - t if reverse else t

    blk = (1, tt, HG_DIM)
    in_specs = [pl.BlockSpec(blk, lambda bi, h, t: (bi, tmap(t), base + h)),
                pl.BlockSpec(blk, lambda bi, h, t: (bi, tmap(t), base + HG_HEADS + h)),
                pl.BlockSpec(blk, lambda bi, h, t: (bi, tmap(t), fcol + h)),
                pl.BlockSpec((1, 1, HG_DIM), lambda bi, h, t: (h, 0, 0)),
                pl.BlockSpec((1, 1, HG_DIM, HG_DIM), lambda bi, h, t: (bi, h, 0, 0)),
                pl.BlockSpec(dmat.shape, lambda bi, h, t: (0, 0)),
                pl.BlockSpec(qsel.shape, lambda bi, h, t: (0, 0, 0)),
                pl.BlockSpec(amask.shape, lambda bi, h, t: (0, 0, 0))]
    args = [mix, mix, fg, lb.reshape(HG_HEADS, 1, HG_DIM), s0, dmat, qsel, amask]
    if fuse:
        in_specs += [pl.BlockSpec(blk, lambda bi, h, t: (bi, tmap(t), h)),
                     pl.BlockSpec(blk, lambda bi, h, t: (bi, tmap(t), base + 2 * HG_HEADS + h)),
                     pl.BlockSpec((1, HG_DIM), lambda bi, h, t: (0, 0))]
        args += [o_fwd, mix, gn.reshape(1, HG_DIM)]
    return pl.pallas_call(
        functools.partial(_hg_kernel, reverse, fuse, tt // HG_CHUNK),
        grid=(b, HG_HEADS, nt),
        in_specs=in_specs,
        out_specs=pl.BlockSpec(blk, lambda bi, h, t: (bi, tmap(t), h)),
        out_shape=jax.ShapeDtypeStruct((b, s, HG_WIDTH), BF16 if fuse else F32),
        scratch_shapes=[pltpu.VMEM((HG_DIM, HG_DIM), F32)],
        compiler_params=_cparams(("parallel", "parallel", "arbitrary")),
        name="hgrn_bwd" if reverse else "hgrn_fwd",
    )(*args)


def _merge_kernel(oa_ref, ob_ref, ga_ref, gb_ref, x_ref, g1_ref, n2_ref, sh2_ref, sc2_ref,
                  wpa_ref, wpb_ref, wo_ref, x1_ref, h2_ref, y_scr):
    tm, d = x1_ref.shape[1:]
    nc = _pick(d, (512, 256, 128))
    oa, ob = oa_ref[0], ob_ref[0]
    for n0 in range(0, d, nc):
        cs = slice(n0, n0 + nc)
        ya = jnp.dot(oa, wpa_ref[:, cs], preferred_element_type=F32)
        yb = jnp.dot(ob, wpb_ref[:, cs], preferred_element_type=F32)
        y_scr[:, cs] = (jax.nn.sigmoid(ga_ref[0, :, cs].astype(F32)) * ya
                        + jax.nn.sigmoid(gb_ref[0, :, cs].astype(F32)) * yb).astype(BF16)
    y = y_scr[...]
    ss = jnp.zeros((tm, 1), F32)
    for n0 in range(0, d, nc):
        cs = slice(n0, n0 + nc)
        x1 = x_ref[0, :, cs] + g1_ref[0, :, cs] * jnp.dot(y, wo_ref[:, cs], preferred_element_type=F32)
        x1_ref[0, :, cs] = x1
        ss = ss + jnp.sum(x1 * x1, axis=-1, keepdims=True)
    r = lax.rsqrt(ss * (1.0 / d) + EPS)
    h2_ref[0] = (x1_ref[0] * r * n2_ref[...] * (1.0 + sc2_ref[0]) + sh2_ref[0]).astype(BF16)


def _merge(o_a, o_b, mix, x, mod, n2, wpa, wpb, wo):
    b, s, d = x.shape
    tm = _pick(s, (512, 256, 128))
    const = lambda shape: pl.BlockSpec(shape, lambda bi, i: (0,) * len(shape), pipeline_mode=pl.Buffered(1))
    vec = lambda j: _mod_spec(d, 2, lambda bi: bi, j)
    return pl.pallas_call(
        _merge_kernel,
        grid=(b, s // tm),
        in_specs=[pl.BlockSpec((1, tm, NA_WIDTH), lambda bi, i: (bi, i, 0)),
                  pl.BlockSpec((1, tm, HG_WIDTH), lambda bi, i: (bi, i, 0)),
                  pl.BlockSpec((1, tm, d), lambda bi, i: (bi, i, 0)),
                  pl.BlockSpec((1, tm, d), lambda bi, i: (bi, i, 1)),
                  pl.BlockSpec((1, tm, d), lambda bi, i: (bi, i, 0)),
                  vec(2), const((1, d)), vec(3), vec(4),
                  const(wpa.shape), const(wpb.shape), const(wo.shape)],
        out_specs=[pl.BlockSpec((1, tm, d), lambda bi, i: (bi, i, 0)),
                   pl.BlockSpec((1, tm, d), lambda bi, i: (bi, i, 0))],
        out_shape=[jax.ShapeDtypeStruct((b, s, d), F32), jax.ShapeDtypeStruct((b, s, d), BF16)],
        scratch_shapes=[pltpu.VMEM((tm, d), BF16)],
        compiler_params=_cparams(("parallel", "parallel")),
        name="merge",
    )(o_a, o_b, mix, mix, x, mod, n2, mod, mod, wpa, wpb, wo)


def _ffn_kernel(nf, nx, h_ref, x1c_ref, g2_ref, fg_ref, wa_ref, wu_ref, wo_ref, o_ref, x1_scr):
    j = pl.program_id(2)
    tm, d = o_ref.shape[1:]
    rows = min(tm, MXU_ROWS_PER_DOT)
    nc = _pick(d, (512, 256, 128))
    xr = tm // nx

    @pl.when(j < nx)
    def _():
        x1_scr[pl.ds(pl.multiple_of(j * xr, xr), xr), :] = x1c_ref[0]

    def hidden_slice(first, last):
        for r0 in range(0, tm, rows):
            rs = slice(r0, r0 + rows)
            h = h_ref[0, rs]
            a = jnp.dot(h, wa_ref[...], preferred_element_type=F32)
            u = jnp.dot(h, wu_ref[...], preferred_element_type=F32)
            gate = (a * jax.nn.sigmoid(a) * u).astype(BF16)
            for n0 in range(0, d, nc):
                part = jnp.dot(gate, wo_ref[:, n0:n0 + nc], preferred_element_type=F32)
                if first:
                    o_ref[0, rs, n0:n0 + nc] = part
                else:
                    o_ref[0, rs, n0:n0 + nc] += part
            if last:
                x2 = x1_scr[rs, :] + g2_ref[0] * o_ref[0, rs]
                ms = jnp.mean(x2 * x2, axis=-1, keepdims=True)
                o_ref[0, rs] = x2 * lax.rsqrt(ms + EPS) * fg_ref[...]

    if nf == 1:
        hidden_slice(True, True)
    else:
        pl.when(j == 0)(functools.partial(hidden_slice, True, False))
        if nf > 2:
            pl.when((j > 0) & (j < nf - 1))(functools.partial(hidden_slice, False, False))
        pl.when(j == nf - 1)(functools.partial(hidden_slice, False, True))


def _ffn(h2, x1, mod, final_g, w_in, w_out):
    b, s, d = x1.shape
    fh = w_out.shape[0]
    tf = _pick(fh, (512, 256, 128))
    nf = fh // tf
    tm = _pick(s, (1024, 512, 256, 128))
    nx = max(n for n in (1, 2, 4, 8) if n <= nf and tm % (8 * n) == 0)
    return pl.pallas_call(
        functools.partial(_ffn_kernel, nf, nx),
        grid=(b, s // tm, nf),
        in_specs=[pl.BlockSpec((1, tm, d), lambda bi, i, j: (bi, i, 0)),
                  pl.BlockSpec((1, tm // nx, d), lambda bi, i, j: (bi, i * nx + jnp.minimum(j, nx - 1), 0)),
                  _mod_spec(d, 3, lambda bi: bi, 5),
                  pl.BlockSpec((1, d), lambda bi, i, j: (0, 0)),
                  pl.BlockSpec((d, tf), lambda bi, i, j: (0, j)),
                  pl.BlockSpec((d, tf), lambda bi, i, j: (0, nf + j)),
                  pl.BlockSpec((tf, d), lambda bi, i, j: (j, 0))],
        out_specs=pl.BlockSpec((1, tm, d), lambda bi, i, j: (bi, i, 0)),
        out_shape=jax.ShapeDtypeStruct((b, s, d), F32),
        scratch_shapes=[pltpu.VMEM((tm, d), F32)],
        compiler_params=_cparams(("parallel", "parallel", "arbitrary")),
        name="ffn",
    )(h2, x1, mod, final_g.reshape(1, d), w_in, w_in, w_out)


def kernel(x, c, ctx, c_ctx, w_ada, b_ada, norm1_g, w_in, na_rpb, hg_lb_logits, hg_norm_g,
           w_pa, w_pb, w_out, norm2_g, w_ffn_in, w_ffn_out, final_g):
    b, s, d = x.shape
    assert w_ada.shape[0] == 1, "single layer"

    lb_table = jnp.cumsum(jax.nn.softmax(hg_lb_logits.astype(F32), axis=0), axis=0)
    lb_f, lb_b = lb_table[0, 0], lb_table[0, 1]

    c8 = jnp.zeros((8, d), F32).at[:b].set(c).at[b].set(c_ctx)
    mod = _ada(c8, w_ada[0], b_ada[0]).reshape(8 * N_MOD, 1, d)

    assert NA_WIDTH == HG_WIDTH == INPROJ_TN and (2 * d) % INPROJ_TN == 0
    q_, k_, v_, hq_, ff_, fb_, hi_, hog_ = range(8)
    gates = list(range(8, 8 + 2 * d // INPROJ_TN))
    w_bf = w_in[0].astype(BF16)
    n1 = norm1_g[0].reshape(1, d)

    qkv, mix, fg = _norm_inproj(x, n1, mod, lambda bi: bi, w_bf,
                                [(BF16, [q_, k_, v_]), (BF16, gates + [hq_, hi_, hog_]), (F32, [ff_, fb_])])
    kv_c, hi_c, fg_c = _norm_inproj(ctx, n1, mod, lambda bi: b, w_bf,
                                    [(BF16, [k_, v_]), (BF16, [hi_]), (F32, [ff_, fb_])])

    o_na = _na(qkv, kv_c, _na_bias_table(na_rpb[0]))

    cf, cb = _hg_constants(False), _hg_constants(True)
    s_f = _hgrn_state(hi_c, fg_c, lb_f, cf[0], False)
    s_b = _hgrn_state(hi_c, fg_c, lb_b, cb[0], True)
    o_f = _hgrn(mix, fg, lb_f, s_f, cf, False, d)
    o_hg = _hgrn(mix, fg, lb_b, s_b, cb, True, d, o_fwd=o_f, gn=hg_norm_g[0])

    x1, h2 = _merge(o_na, o_hg, mix, x, mod, norm2_g[0].reshape(1, d),
                    w_pa[0].astype(BF16), w_pb[0].astype(BF16), w_out[0].astype(BF16))
    return _ffn(h2, x1, mod, final_g, w_ffn_in[0].astype(BF16), w_ffn_out[0].astype(BF16))
```

```python
import functools

import jax
import jax.numpy as jnp
import numpy as np
from jax import lax
from jax.experimental import pallas as pl
from jax.experimental.pallas import tpu as pltpu

F32 = jnp.float32
BF16 = jnp.bfloat16

GRID_W = 64
WIN_H = 8
WIN_W = 16
NA_HEADS = 16
NA_HEAD_DIM = 64
NA_WIDTH = NA_HEADS * NA_HEAD_DIM
HG_HEADS = 8
HG_DIM = 128
HG_WIDTH = HG_HEADS * HG_DIM
N_MOD = 6
EPS = 1e-6
NEG_BIG = -1e30

V7X_VMEM_LIMIT_BYTES = 60000 * 1024
MXU_ROWS_PER_DOT = 512
NA_SLAB_HEADS = 4
NA_ROWS_PER_STEP = 4
HG_CHUNK = 64
HG_LEVELS = 6
HG_VPU_LEVELS = 1
LOG2_E = 1.4426950408889634


def _cparams(sem):
    return pltpu.CompilerParams(dimension_semantics=sem, vmem_limit_bytes=V7X_VMEM_LIMIT_BYTES)


def _pick(n, prefs):
    for p in prefs:
        if n % p == 0:
            return p
    return n


def _ada_kernel(c_ref, w_ref, b_ref, o_ref):
    c = c_ref[...]
    s = c * jax.nn.sigmoid(c)
    o_ref[...] = jnp.dot(s, w_ref[...], precision=lax.Precision.HIGHEST,
                         preferred_element_type=F32) + b_ref[...]


def _ada(c8, w, b):
    d, n = w.shape
    tn = _pick(n, (1024, 512, 256, 128))
    return pl.pallas_call(
        _ada_kernel,
        grid=(n // tn,),
        in_specs=[pl.BlockSpec((8, d), lambda j: (0, 0)),
                  pl.BlockSpec((d, tn), lambda j: (0, j)),
                  pl.BlockSpec((1, tn), lambda j: (0, j))],
        out_specs=pl.BlockSpec((8, tn), lambda j: (0, j)),
        out_shape=jax.ShapeDtypeStruct((8, n), F32),
        compiler_params=_cparams(("arbitrary",)),
        name="ada",
    )(c8, w, b.reshape(1, n))


INPROJ_TN = 1024
NORM_CHUNKS = (8, 4, 2, 1)


def _norm_inproj_kernel(starts, nx, x_ref, g_ref, sh_ref, sc_ref, w_ref, *refs):
    o_refs, h_bufs = refs[:-2], refs[-2:]
    i, j = pl.program_id(1), pl.program_id(2)
    tm = h_bufs[0].shape[0]
    xr = tm // nx
    rows = min(tm, MXU_ROWS_PER_DOT)

    def normalise_chunk(h_fill):
        x = x_ref[0]
        ms = jnp.mean(x * x, axis=-1, keepdims=True)
        y = x * lax.rsqrt(ms + EPS) * g_ref[...]
        r0 = pl.multiple_of(jnp.minimum(j, nx - 1) * xr, xr)
        h_fill[pl.ds(r0, xr), :] = (y * (1.0 + sc_ref[0]) + sh_ref[0]).astype(BF16)

    def project(o_ref, h_fill, h_use):
        normalise_chunk(h_fill)
        for r0 in range(0, tm, rows):
            o_ref[0, r0:r0 + rows] = jnp.dot(h_use[r0:r0 + rows], w_ref[...],
                                             preferred_element_type=F32).astype(o_ref.dtype)

    pl.when(i == 0)(functools.partial(normalise_chunk, h_bufs[0]))
    for parity in (0, 1):
        for k, o_ref in enumerate(o_refs):
            pl.when((i > 0) & (i % 2 == parity) & (j >= starts[k]) & (j < starts[k + 1]))(
                functools.partial(project, o_ref, h_bufs[parity], h_bufs[1 - parity]))


def _mod_spec(d, grid_rank, row, j):
    if grid_rank == 2:
        return pl.BlockSpec((1, 1, d), lambda bi, i: (row(bi) * N_MOD + j, 0, 0))
    return pl.BlockSpec((1, 1, d), lambda bi, i, k: (row(bi) * N_MOD + j, 0, 0))


def _norm_inproj(x, g, mod, row, w, outputs):
    b, t, d = x.shape
    tn = INPROJ_TN
    tm = _pick(t, (1024, 512, 256))
    ni = t // tm
    counts = [len(blocks) for _, blocks in outputs]
    starts = [int(v) for v in np.cumsum([0] + counts)]
    nj = starts[-1]
    table = [blk for _, blocks in outputs for blk in blocks]
    nx = next(n for n in NORM_CHUNKS if n <= nj and tm % (16 * n) == 0)
    xr = tm // nx

    def w_block(i, j):
        r = jnp.int32(table[-1])
        for idx in range(len(table) - 2, -1, -1):
            r = jnp.where(j == idx, table[idx], r)
        return jnp.where(i == 0, table[0], r)

    def x_block(i, j):
        return jnp.where(i >= ni, ni * nx - 1, i * nx + jnp.minimum(j, nx - 1))

    def o_spec(k):
        return pl.BlockSpec((1, tm, tn), lambda bi, i, j: (
            bi, jnp.maximum(i - 1, 0), jnp.where(i == 0, 0, jnp.clip(j - starts[k], 0, counts[k] - 1))))

    return pl.pallas_call(
        functools.partial(_norm_inproj_kernel, starts, nx),
        grid=(b, ni + 1, nj),
        in_specs=[pl.BlockSpec((1, xr, d), lambda bi, i, j: (bi, x_block(i, j), 0)),
                  pl.BlockSpec((1, d), lambda bi, i, j: (0, 0)),
                  _mod_spec(d, 3, row, 0), _mod_spec(d, 3, row, 1),
                  pl.BlockSpec((d, tn), lambda bi, i, j: (0, w_block(i, j)))],
        out_specs=[o_spec(k) for k in range(len(outputs))],
        out_shape=[jax.ShapeDtypeStruct((b, t, n * tn), dt) for (dt, _), n in zip(outputs, counts)],
        scratch_shapes=[pltpu.VMEM((tm, d), BF16), pltpu.VMEM((tm, d), BF16)],
        compiler_params=_cparams(("parallel", "arbitrary", "arbitrary")),
        name="norm_inproj",
    )(x, g, mod, mod, w)


def _na_kernel(nb, nr, q_ref, *refs):
    kb_refs, vb_refs, (kc_ref, vc_ref), bias_refs, o_ref = (refs[:nr], refs[nr:2 * nr], refs[2 * nr:2 * nr + 2],
                                                            refs[2 * nr + 2:3 * nr + 2], refs[3 * nr + 2])
    dn = (((1,), (1,)), ((), ()))
    nh, w = NA_SLAB_HEADS, NA_SLAB_HEADS * NA_HEAD_DIM
    hq = nh * GRID_W
    lane_head = lax.broadcasted_iota(jnp.int32, (GRID_W, w), 1) // NA_HEAD_DIM
    own = [lane_head == a for a in range(nh)]
    slabs = range(NA_HEADS // nh)
    work = [(j, g) for j in range(nr) for g in slabs]
    cols = {g: slice(g * w, (g + 1) * w) for g in slabs}
    q = {}
    for j, g in work:
        qg = q_ref[0, j * GRID_W:(j + 1) * GRID_W, cols[g]] * (NA_HEAD_DIM ** -0.5)
        q[j, g] = jnp.concatenate([jnp.where(own[a], qg, jnp.zeros_like(qg)) for a in range(nh)], axis=0)
    s_b = {(j, g): lax.dot_general(q[j, g], kb_refs[j][0, :, cols[g]], dn, preferred_element_type=F32)
           + bias_refs[j][0, g * nh:(g + 1) * nh].reshape(hq, nb) for j, g in work}
    s_c = {}
    for g in slabs:
        both = lax.dot_general(jnp.concatenate([q[j, g] for j in range(nr)], axis=0), kc_ref[0, :, cols[g]], dn,
                               preferred_element_type=F32)
        for j in range(nr):
            s_c[j, g] = both[j * hq:(j + 1) * hq]
    m = {k: jnp.maximum(jnp.max(s_b[k], axis=-1, keepdims=True), jnp.max(s_c[k], axis=-1, keepdims=True))
         for k in work}
    p_b = {k: jnp.exp(s_b[k] - m[k]) for k in work}
    p_c = {k: jnp.exp(s_c[k] - m[k]) for k in work}
    l = {k: jnp.sum(p_b[k], axis=-1, keepdims=True) + jnp.sum(p_c[k], axis=-1, keepdims=True) for k in work}
    o_c = {}
    for g in slabs:
        both = jnp.dot(jnp.concatenate([p_c[j, g] for j in range(nr)], axis=0).astype(BF16), vc_ref[0, :, cols[g]],
                       preferred_element_type=F32)
        for j in range(nr):
            o_c[j, g] = both[j * hq:(j + 1) * hq]
    o = {(j, g): (jnp.dot(p_b[j, g].astype(BF16), vb_refs[j][0, :, cols[g]], preferred_element_type=F32)
                  + o_c[j, g]) / l[j, g] for j, g in work}
    for j, g in work:
        out = o[j, g][0:GRID_W]
        for a in range(1, nh):
            out = jnp.where(own[a], o[j, g][a * GRID_W:(a + 1) * GRID_W], out)
        o_ref[0, j * GRID_W:(j + 1) * GRID_W, cols[g]] = out.astype(BF16)


def _na_bias_table(rpb):
    col = np.arange(GRID_W)
    col_start = np.clip(col - WIN_W // 2, 0, GRID_W - WIN_W)
    in_win = (col[None, :] >= col_start[:, None]) & (col[None, :] < col_start[:, None] + WIN_W)
    dc_idx = np.clip(col[None, :] - col[:, None], 1 - WIN_W, WIN_W - 1) + WIN_W - 1
    pick_col = (dc_idx[None] == np.arange(2 * WIN_W - 1)[:, None, None]).astype(np.float32)
    off, row = np.arange(WIN_H)[:, None, None], np.arange(WIN_H)[None, :, None]
    pick_row = (off + row == np.arange(2 * WIN_H - 1)[None, None, :]).astype(np.float32)
    by_row = jnp.einsum('oid,hdc->ohic', jnp.asarray(pick_row), rpb.astype(F32), precision=lax.Precision.HIGHEST)
    tab = jnp.einsum('ohic,cqk->ohqik', by_row, jnp.asarray(pick_col), precision=lax.Precision.HIGHEST)
    tab = jnp.where(jnp.asarray(in_win)[None, None, :, None, :], tab, NEG_BIG)
    return tab.reshape(WIN_H, NA_HEADS, GRID_W, WIN_H * GRID_W)


def _na(qkv, kv_c, bias_tab):
    b, s, _ = qkv.shape
    lc = kv_c.shape[1]
    rows = s // GRID_W
    nr = _pick(rows, (NA_ROWS_PER_STEP, 1))
    assert rows >= WIN_H
    nb = WIN_H * GRID_W

    def rs_of(r):
        return jnp.clip(r - WIN_H // 2, 0, rows - WIN_H)

    band = (pl.Element(1), pl.Element(nb), pl.Element(NA_WIDTH))
    k_band = [pl.BlockSpec(band, lambda bi, rr, j=j: (bi, rs_of(rr * nr + j) * GRID_W, NA_WIDTH))
              for j in range(nr)]
    v_band = [pl.BlockSpec(band, lambda bi, rr, j=j: (bi, rs_of(rr * nr + j) * GRID_W, 2 * NA_WIDTH))
              for j in range(nr)]
    bias = [pl.BlockSpec((1, NA_HEADS, GRID_W, nb),
                         lambda bi, rr, j=j: (rs_of(rr * nr + j) - (rr * nr + j) + WIN_H - 1, 0, 0, 0))
            for j in range(nr)]
    return pl.pallas_call(
        functools.partial(_na_kernel, nb, nr),
        grid=(b, rows // nr),
        in_specs=[pl.BlockSpec((1, nr * GRID_W, NA_WIDTH), lambda bi, rr: (bi, rr, 0)), *k_band, *v_band,
                  pl.BlockSpec((1, lc, NA_WIDTH), lambda bi, rr: (bi, 0, 0)),
                  pl.BlockSpec((1, lc, NA_WIDTH), lambda bi, rr: (bi, 0, 1)), *bias],
        out_specs=pl.BlockSpec((1, nr * GRID_W, NA_WIDTH), lambda bi, rr: (bi, rr, 0)),
        out_shape=jax.ShapeDtypeStruct((b, s, NA_WIDTH), BF16),
        compiler_params=_cparams(("parallel", "arbitrary")),
        name="na",
    )(qkv, *([qkv] * (2 * nr)), kv_c, kv_c, *([bias_tab] * nr))


def _hg_constants(reverse):
    c = HG_CHUNK
    pos = np.arange(c)
    pmat = np.tril(np.ones((c, c), np.float32))
    qsel = np.zeros((HG_LEVELS, c, HG_DIM), np.float32)
    amask = np.zeros((HG_LEVELS - 1, c, c), np.float32)
    for l in range(HG_LEVELS):
        m = 1 << l
        blk = pos // (2 * m)
        is_q = (pos % (2 * m)) >= m
        qsel[l] = is_q[:, None]
        if l > 0:
            amask[l - 1] = (blk[:, None] == blk[None, :]) & is_q[:, None] & ~is_q[None, :]
    if reverse:
        pmat = pmat[::-1, ::-1]
        qsel = qsel[:, ::-1]
        amask = amask[:, ::-1, ::-1]
    return (jnp.asarray(np.concatenate([pmat, pmat, pmat], axis=1), BF16),
            jnp.asarray(qsel, F32), jnp.asarray(amask, F32))


def _boundary_rows(p, l, reverse):
    c = HG_CHUNK
    m = 1 << l
    off = m if reverse else m - 1
    if 2 * m >= 16:
        return jnp.concatenate([jnp.broadcast_to(p[b0 + off:b0 + off + 1], (2 * m, HG_DIM))
                                for b0 in range(0, c, 2 * m)], axis=0)
    p3 = p.reshape(c // 8, 8, HG_DIM)
    pick = lambda s: jnp.broadcast_to(p3[:, s:s + 1], p3.shape)
    if 2 * m == 8:
        return pick(off).reshape(c, HG_DIM)
    assert 2 * m == 4
    upper = lax.broadcasted_iota(jnp.int32, p3.shape, 1) >= 4
    return jnp.where(upper, pick(4 + off), pick(off)).reshape(c, HG_DIM)


def _chunk_prefixes(pmat, pieces, order):
    c = HG_CHUNK
    rows = lambda a, ci: a[ci * c:(ci + 1) * c]
    pre = {}
    for ca, cb in zip(order[0::2], order[1::2]):
        rhs = jnp.concatenate([jnp.concatenate([rows(a, ca), rows(a, cb)], axis=1) for a in pieces], axis=0)
        both = jnp.dot(pmat, rhs, preferred_element_type=F32)
        pre[ca], pre[cb] = both[:, :HG_DIM], both[:, HG_DIM:]
    if len(order) % 2:
        ci = order[-1]
        pre[ci] = jnp.dot(pmat, jnp.concatenate([rows(a, ci) for a in pieces], axis=0),
                          preferred_element_type=F32)
    return pre


def _split3(a):
    hi = a.astype(BF16)
    r1 = a - hi.astype(F32)
    mid = r1.astype(BF16)
    lo = (r1 - mid.astype(F32)).astype(BF16)
    return hi, mid, lo


def _hg_block(q, z, v, lb, st, pmat, qsel_ref, amask_ref, order, reverse):
    c = HG_CHUNK
    dn_nt = (((1,), (1,)), ((), ()))
    f = lb + (1.0 - lb) * jax.nn.sigmoid(z)
    kk = 1.0 - f
    qf = q.astype(F32)
    rows = lambda a, ci: a[ci * c:(ci + 1) * c]
    pre = _chunk_prefixes(pmat, _split3(jnp.log(f)), order)
    qsel = [qsel_ref[l] != 0.0 for l in range(HG_LEVELS)]
    amask = [amask_ref[l] != 0.0 for l in range(HG_LEVELS - 1)]
    a = {}
    for l in range(HG_VPU_LEVELS, HG_LEVELS):
        sign = jnp.where(qsel[l], LOG2_E, -LOG2_E)
        w = {ci: (jnp.exp2((pre[ci] - _boundary_rows(pre[ci], l, reverse)) * sign)
                  * jnp.where(qsel[l], rows(qf, ci), rows(kk, ci))).astype(BF16) for ci in order}
        p = {ci: lax.dot_general(w[ci], w[ci], dn_nt, preferred_element_type=F32) for ci in order}
        a = {ci: jnp.where(amask[l - 1], p[ci], a[ci] if l > HG_VPU_LEVELS else 0.0) for ci in order}
    last = 0 if reverse else c - 1
    e_in = {ci: jnp.exp(pre[ci]) for ci in order}
    e_out = {ci: jnp.exp(pre[ci][last:last + 1] - pre[ci]) for ci in order}
    upd = {ci: lax.dot_general(rows(v, ci), (rows(kk, ci) * e_out[ci]).astype(BF16), (((0,), (0,)), ((), ())),
                               preferred_element_type=F32) for ci in order}
    o_intra = {ci: jnp.dot(a[ci].astype(BF16), rows(v, ci), preferred_element_type=F32) for ci in order}
    vf = v.astype(F32)
    small = 1 << HG_VPU_LEVELS
    row = lax.broadcasted_iota(jnp.int32, (c, 1), 0)
    in_block = ((c - 1 - row) if reverse else row) % small
    back = lambda x, dist: pltpu.roll(x, (c - dist) if reverse else dist, 0)
    for ci in order:
        qc, kc, vc, fc = rows(qf, ci), rows(kk, ci), rows(vf, ci), rows(f, ci)
        acc = o_intra[ci] + jnp.sum(qc * kc, axis=-1, keepdims=True) * vc
        decay = fc
        for dist in range(1, small):
            if dist > 1:
                decay = decay * back(fc, dist - 1)
            wgt = jnp.sum(qc * decay * back(kc, dist), axis=-1, keepdims=True)
            acc = acc + jnp.where(in_block >= dist, wgt, 0.0) * back(vc, dist)
        o_intra[ci] = acc
    qd = {ci: (rows(qf, ci) * e_in[ci]).astype(BF16) for ci in order}
    out = {}
    for ci in order:
        out[ci] = o_intra[ci] + lax.dot_general(qd[ci], st.astype(BF16), dn_nt, preferred_element_type=F32)
        st = st * e_in[ci][last:last + 1] + upd[ci]
    return out, st


def _hg_kernel(reverse, fuse_readout, nchunk, *refs):
    if fuse_readout:
        (q_ref, v_ref, z_ref, lb_ref, s0_ref, dmat_ref, qsel_ref, amask_ref, of_ref, og_ref, gn_ref,
         o_ref, st_scr) = refs
    else:
        (q_ref, v_ref, z_ref, lb_ref, s0_ref, dmat_ref, qsel_ref, amask_ref,
         o_ref, st_scr) = refs
    t = pl.program_id(2)

    @pl.when(t == 0)
    def _():
        st_scr[...] = s0_ref[0, 0]

    order = list(range(nchunk - 1, -1, -1) if reverse else range(nchunk))
    out, st = _hg_block(q_ref[0], z_ref[0], v_ref[0], lb_ref[0], st_scr[...], dmat_ref[...],
                        qsel_ref, amask_ref, order, reverse)
    o = jnp.concatenate([out[ci] for ci in range(nchunk)], axis=0)
    if fuse_readout:
        o = o + of_ref[0]
        on = o * lax.rsqrt(jnp.mean(o * o, axis=-1, keepdims=True) + EPS) * gn_ref[...]
        g = og_ref[0].astype(F32)
        o_ref[0] = (on * (g * jax.nn.sigmoid(g))).astype(o_ref.dtype)
    else:
        o_ref[0] = o
    st_scr[...] = st


def _hg_state_kernel(reverse, nchunk, v_ref, z_ref, lb_ref, pmat_ref, sout_ref):
    c = HG_CHUNK
    v = v_ref[0]
    f = lb_ref[...] + (1.0 - lb_ref[...]) * jax.nn.sigmoid(z_ref[0])
    kk = 1.0 - f
    pieces = _split3(jnp.log(f))
    order = list(range(nchunk - 1, -1, -1) if reverse else range(nchunk))
    last = 0 if reverse else c - 1
    tile = lambda a, h, ci: a[ci * c:(ci + 1) * c, h * HG_DIM:(h + 1) * HG_DIM]
    pre, upd = {}, {}
    for h in range(HG_HEADS):
        head = [a[:, h * HG_DIM:(h + 1) * HG_DIM] for a in pieces]
        for ci, p in _chunk_prefixes(pmat_ref[...], head, order).items():
            pre[h, ci] = p
    for k, p in pre.items():
        e_out = jnp.exp(p[last:last + 1] - p)
        upd[k] = lax.dot_general(tile(v, *k), (tile(kk, *k) * e_out).astype(BF16), (((0,), (0,)), ((), ())),
                                 preferred_element_type=F32)
    for h in range(HG_HEADS):
        st = jnp.zeros((HG_DIM, HG_DIM), F32)
        for ci in order:
            st = st * jnp.exp(pre[h, ci][last:last + 1]) + upd[h, ci]
        sout_ref[0, h] = st


def _hgrn_state(val, fg, lb, dmat, reverse):
    b, l, _ = val.shape
    assert l % HG_CHUNK == 0
    fblk = 1 if reverse else 0
    return pl.pallas_call(
        functools.partial(_hg_state_kernel, reverse, l // HG_CHUNK),
        grid=(b,),
        in_specs=[pl.BlockSpec((1, l, HG_WIDTH), lambda bi: (bi, 0, 0)),
                  pl.BlockSpec((1, l, HG_WIDTH), lambda bi: (bi, 0, fblk)),
                  pl.BlockSpec((1, HG_WIDTH), lambda bi: (0, 0)),
                  pl.BlockSpec(dmat.shape, lambda bi: (0, 0))],
        out_specs=pl.BlockSpec((1, HG_HEADS, HG_DIM, HG_DIM), lambda bi: (bi, 0, 0, 0)),
        out_shape=jax.ShapeDtypeStruct((b, HG_HEADS, HG_DIM, HG_DIM), F32),
        compiler_params=_cparams(("parallel",)),
        name="hgrn_state_bwd" if reverse else "hgrn_state_fwd",
    )(val, fg, lb.reshape(1, HG_WIDTH), dmat)


def _hgrn(mix, fg, lb, s0, consts, reverse, d, o_fwd=None, gn=None):
    b, s, _ = mix.shape
    tt = _pick(s, (4096, 2048, 1024, 512, 256, 128, 64))
    nt = s // tt
    base = 2 * d // HG_DIM
    fuse = o_fwd is not None
    dmat, qsel, amask = consts
    fcol = HG_HEADS if reverse else 0

    def tmap(t):
        return nt - 1 - t if reverse else t

    blk = (1, tt, HG_DIM)
    in_specs = [pl.BlockSpec(blk, lambda bi, h, t: (bi, tmap(t), base + h)),
                pl.BlockSpec(blk, lambda bi, h, t: (bi, tmap(t), base + HG_HEADS + h)),
                pl.BlockSpec(blk, lambda bi, h, t: (bi, tmap(t), fcol + h)),
                pl.BlockSpec((1, 1, HG_DIM), lambda bi, h, t: (h, 0, 0)),
                pl.BlockSpec((1, 1, HG_DIM, HG_DIM), lambda bi, h, t: (bi, h, 0, 0)),
                pl.BlockSpec(dmat.shape, lambda bi, h, t: (0, 0)),
                pl.BlockSpec(qsel.shape, lambda bi, h, t: (0, 0, 0)),
                pl.BlockSpec(amask.shape, lambda bi, h, t: (0, 0, 0))]
    args = [mix, mix, fg, lb.reshape(HG_HEADS, 1, HG_DIM), s0, dmat, qsel, amask]
    if fuse:
        in_specs += [pl.BlockSpec(blk, lambda bi, h, t: (bi, tmap(t), h)),
                     pl.BlockSpec(blk, lambda bi, h, t: (bi, tmap(t), base + 2 * HG_HEADS + h)),
                     pl.BlockSpec((1, HG_DIM), lambda bi, h, t: (0, 0))]
        args += [o_fwd, mix, gn.reshape(1, HG_DIM)]
    return pl.pallas_call(
        functools.partial(_hg_kernel, reverse, fuse, tt // HG_CHUNK),
        grid=(b, HG_HEADS, nt),
        in_specs=in_specs,
        out_specs=pl.BlockSpec(blk, lambda bi, h, t: (bi, tmap(t), h)),
        out_shape=jax.ShapeDtypeStruct((b, s, HG_WIDTH), BF16 if fuse else F32),
        scratch_shapes=[pltpu.VMEM((HG_DIM, HG_DIM), F32)],
        compiler_params=_cparams(("parallel", "parallel", "arbitrary")),
        name="hgrn_bwd" if reverse else "hgrn_fwd",
    )(*args)


def _merge_kernel(oa_ref, ob_ref, ga_ref, gb_ref, x_ref, g1_ref, n2_ref, sh2_ref, sc2_ref,
                  wpa_ref, wpb_ref, wo_ref, x1_ref, h2_ref, y_scr):
    tm, d = x1_ref.shape[1:]
    nc = _pick(d, (512, 256, 128))
    oa, ob = oa_ref[0], ob_ref[0]
    for n0 in range(0, d, nc):
        cs = slice(n0, n0 + nc)
        ya = jnp.dot(oa, wpa_ref[:, cs], preferred_element_type=F32)
        yb = jnp.dot(ob, wpb_ref[:, cs], preferred_element_type=F32)
        y_scr[:, cs] = (jax.nn.sigmoid(ga_ref[0, :, cs].astype(F32)) * ya
                        + jax.nn.sigmoid(gb_ref[0, :, cs].astype(F32)) * yb).astype(BF16)
    y = y_scr[...]
    ss = jnp.zeros((tm, 1), F32)
    for n0 in range(0, d, nc):
        cs = slice(n0, n0 + nc)
        x1 = x_ref[0, :, cs] + g1_ref[0, :, cs] * jnp.dot(y, wo_ref[:, cs], preferred_element_type=F32)
        x1_ref[0, :, cs] = x1
        ss = ss + jnp.sum(x1 * x1, axis=-1, keepdims=True)
    r = lax.rsqrt(ss * (1.0 / d) + EPS)
    h2_ref[0] = (x1_ref[0] * r * n2_ref[...] * (1.0 + sc2_ref[0]) + sh2_ref[0]).astype(BF16)


def _merge(o_a, o_b, mix, x, mod, n2, wpa, wpb, wo):
    b, s, d = x.shape
    tm = _pick(s, (512, 256, 128))
    const = lambda shape: pl.BlockSpec(shape, lambda bi, i: (0,) * len(shape), pipeline_mode=pl.Buffered(1))
    vec = lambda j: _mod_spec(d, 2, lambda bi: bi, j)
    return pl.pallas_call(
        _merge_kernel,
        grid=(b, s // tm),
        in_specs=[pl.BlockSpec((1, tm, NA_WIDTH), lambda bi, i: (bi, i, 0)),
                  pl.BlockSpec((1, tm, HG_WIDTH), lambda bi, i: (bi, i, 0)),
                  pl.BlockSpec((1, tm, d), lambda bi, i: (bi, i, 0)),
                  pl.BlockSpec((1, tm, d), lambda bi, i: (bi, i, 1)),
                  pl.BlockSpec((1, tm, d), lambda bi, i: (bi, i, 0)),
                  vec(2), const((1, d)), vec(3), vec(4),
                  const(wpa.shape), const(wpb.shape), const(wo.shape)],
        out_specs=[pl.BlockSpec((1, tm, d), lambda bi, i: (bi, i, 0)),
                   pl.BlockSpec((1, tm, d), lambda bi, i: (bi, i, 0))],
        out_shape=[jax.ShapeDtypeStruct((b, s, d), F32), jax.ShapeDtypeStruct((b, s, d), BF16)],
        scratch_shapes=[pltpu.VMEM((tm, d), BF16)],
        compiler_params=_cparams(("parallel", "parallel")),
        name="merge",
    )(o_a, o_b, mix, mix, x, mod, n2, mod, mod, wpa, wpb, wo)


def _ffn_kernel(nf, nx, h_ref, x1c_ref, g2_ref, fg_ref, wa_ref, wu_ref, wo_ref, o_ref, x1_scr):
    j = pl.program_id(2)
    tm, d = o_ref.shape[1:]
    rows = min(tm, MXU_ROWS_PER_DOT)
    nc = _pick(d, (512, 256, 128))
    xr = tm // nx

    @pl.when(j < nx)
    def _():
        x1_scr[pl.ds(pl.multiple_of(j * xr, xr), xr), :] = x1c_ref[0]

    def hidden_slice(first, last):
        for r0 in range(0, tm, rows):
            rs = slice(r0, r0 + rows)
            h = h_ref[0, rs]
            a = jnp.dot(h, wa_ref[...], preferred_element_type=F32)
            u = jnp.dot(h, wu_ref[...], preferred_element_type=F32)
            gate = (a * jax.nn.sigmoid(a) * u).astype(BF16)
            for n0 in range(0, d, nc):
                part = jnp.dot(gate, wo_ref[:, n0:n0 + nc], preferred_element_type=F32)
                if first:
                    o_ref[0, rs, n0:n0 + nc] = part
                else:
                    o_ref[0, rs, n0:n0 + nc] += part
            if last:
                x2 = x1_scr[rs, :] + g2_ref[0] * o_ref[0, rs]
                ms = jnp.mean(x2 * x2, axis=-1, keepdims=True)
                o_ref[0, rs] = x2 * lax.rsqrt(ms + EPS) * fg_ref[...]

    if nf == 1:
        hidden_slice(True, True)
    else:
        pl.when(j == 0)(functools.partial(hidden_slice, True, False))
        if nf > 2:
            pl.when((j > 0) & (j < nf - 1))(functools.partial(hidden_slice, False, False))
        pl.when(j == nf - 1)(functools.partial(hidden_slice, False, True))


def _ffn(h2, x1, mod, final_g, w_in, w_out):
    b, s, d = x1.shape
    fh = w_out.shape[0]
    tf = _pick(fh, (512, 256, 128))
    nf = fh // tf
    tm = _pick(s, (1024, 512, 256, 128))
    nx = max(n for n in (1, 2, 4, 8) if n <= nf and tm % (8 * n) == 0)
    return pl.pallas_call(
        functools.partial(_ffn_kernel, nf, nx),
        grid=(b, s // tm, nf),
        in_specs=[pl.BlockSpec((1, tm, d), lambda bi, i, j: (bi, i, 0)),
                  pl.BlockSpec((1, tm // nx, d), lambda bi, i, j: (bi, i * nx + jnp.minimum(j, nx - 1), 0)),
                  _mod_spec(d, 3, lambda bi: bi, 5),
                  pl.BlockSpec((1, d), lambda bi, i, j: (0, 0)),
                  pl.BlockSpec((d, tf), lambda bi, i, j: (0, j)),
                  pl.BlockSpec((d, tf), lambda bi, i, j: (0, nf + j)),
                  pl.BlockSpec((tf, d), lambda bi, i, j: (j, 0))],
        out_specs=pl.BlockSpec((1, tm, d), lambda bi, i, j: (bi, i, 0)),
        out_shape=jax.ShapeDtypeStruct((b, s, d), F32),
        scratch_shapes=[pltpu.VMEM((tm, d), F32)],
        compiler_params=_cparams(("parallel", "parallel", "arbitrary")),
        name="ffn",
    )(h2, x1, mod, final_g.reshape(1, d), w_in, w_in, w_out)


def kernel(x, c, ctx, c_ctx, w_ada, b_ada, norm1_g, w_in, na_rpb, hg_lb_logits, hg_norm_g,
           w_pa, w_pb, w_out, norm2_g, w_ffn_in, w_ffn_out, final_g):
    b, s, d = x.shape
    assert w_ada.shape[0] == 1, "single layer"

    lb_table = jnp.cumsum(jax.nn.softmax(hg_lb_logits.astype(F32), axis=0), axis=0)
    lb_f, lb_b = lb_table[0, 0], lb_table[0, 1]

    c8 = jnp.zeros((8, d), F32).at[:b].set(c).at[b].set(c_ctx)
    mod = _ada(c8, w_ada[0], b_ada[0]).reshape(8 * N_MOD, 1, d)

    assert NA_WIDTH == HG_WIDTH == INPROJ_TN and (2 * d) % INPROJ_TN == 0
    q_, k_, v_, hq_, ff_, fb_, hi_, hog_ = range(8)
    gates = list(range(8, 8 + 2 * d // INPROJ_TN))
    w_bf = w_in[0].astype(BF16)
    n1 = norm1_g[0].reshape(1, d)

    qkv, mix, fg = _norm_inproj(x, n1, mod, lambda bi: bi, w_bf,
                                [(BF16, [q_, k_, v_]), (BF16, gates + [hq_, hi_, hog_]), (F32, [ff_, fb_])])
    kv_c, hi_c, fg_c = _norm_inproj(ctx, n1, mod, lambda bi: b, w_bf,
                                    [(BF16, [k_, v_]), (BF16, [hi_]), (F32, [ff_, fb_])])

    o_na = _na(qkv, kv_c, _na_bias_table(na_rpb[0]))

    cf, cb = _hg_constants(False), _hg_constants(True)
    s_f = _hgrn_state(hi_c, fg_c, lb_f, cf[0], False)
    s_b = _hgrn_state(hi_c, fg_c, lb_b, cb[0], True)
    o_f = _hgrn(mix, fg, lb_f, s_f, cf, False, d)
    o_hg = _hgrn(mix, fg, lb_b, s_b, cb, True, d, o_fwd=o_f, gn=hg_norm_g[0])

    x1, h2 = _merge(o_na, o_hg, mix, x, mod, norm2_g[0].reshape(1, d),
                    w_pa[0].astype(BF16), w_pb[0].astype(BF16), w_out[0].astype(BF16))
    return _ffn(h2, x1, mod, final_g, w_ffn_in[0].astype(BF16), w_ffn_out[0].astype(BF16))
```
